```python
import jax, jax.numpy as jnp
from jax import lax
import numpy as np

D_MODEL = 2048
BATCH = 4
SEQ = 2048
DEPTH = 4
DEC_BATCH = 128
DEC_SEQ = 1
PAST_LEN = 16384
PAGE_SIZE = 128

N_AB = (DEPTH + 1) // 2
N_ML = DEPTH // 2
D_RNN = D_MODEL // 2
D_CONV = D_MODEL // 2
RNN_HEADS = 16
RNN_HEAD_DIM = D_RNN // RNN_HEADS
RNN_CONV_W = 4
LRU_C = 8.0
CONF_CONV_W = 31
ML_HEADS = 8
ML_QK = D_MODEL // 2
ML_V = D_MODEL
ML_DK = ML_QK // ML_HEADS
ML_DV = ML_V // ML_HEADS
ML_CHUNK = 128
D_FF = 4 * D_MODEL
EPS = 1e-6

kernel_name = "hawk_conformer_mlstm_hybrid_step"


def rmsnorm(x, g):
    xf = x.astype(jnp.float32)
    y = xf * lax.rsqrt(jnp.mean(xf * xf, axis=-1, keepdims=True) + EPS)
    return (y * g.astype(jnp.float32)).astype(x.dtype)


def layernorm(x, g, b):
    xf = x.astype(jnp.float32)
    mu = jnp.mean(xf, axis=-1, keepdims=True)
    var = jnp.mean(jnp.square(xf - mu), axis=-1, keepdims=True)
    y = (xf - mu) * lax.rsqrt(var + EPS)
    return (y * g.astype(jnp.float32) + b.astype(jnp.float32)).astype(x.dtype)


def causal_dwconv(x, buf, w, b):
    W, C = w.shape
    xp = jnp.concatenate([buf.astype(x.dtype), x], axis=1)
    y = lax.conv_general_dilated(xp, w.astype(x.dtype)[:, None, :], window_strides=(1,), padding='VALID',
                                 dimension_numbers=('NWC', 'WIO', 'NWC'), feature_group_count=C)
    return y + b.astype(x.dtype), xp[:, xp.shape[1] - (W - 1):]


def linear_scan(a, u, h0):
    def comb(l, r):
        return l[0] * r[0], r[0] * l[1] + r[1]
    A, Bc = lax.associative_scan(comb, (a, u), axis=1)
    h = A * h0[:, None, :] + Bc
    return h, h[:, -1]


def rglru_conformer_mixer(xn, w_in, conv_w, conv_b, wa, ba, wx, bx, lam, dw_w, dw_b, ln_g, ln_b, w_out,
                          rconv_buf, h0, cconv_buf):
    B, T, _ = xn.shape
    f32 = jnp.float32
    proj = jnp.einsum('btd,de->bte', xn, w_in)
    x_r, g_r, a_c, b_c = jnp.split(proj, [D_RNN, 2 * D_RNN, 2 * D_RNN + D_CONV], axis=-1)
    xc, new_rconv = causal_dwconv(x_r, rconv_buf, conv_w, conv_b)
    xh = xc.reshape(B, T, RNN_HEADS, RNN_HEAD_DIM)
    r = jax.nn.sigmoid(jnp.einsum('bthi,hij->bthj', xh, wa).reshape(B, T, D_RNN) + ba)
    ig = jax.nn.sigmoid(jnp.einsum('bthi,hij->bthj', xh, wx).reshape(B, T, D_RNN) + bx)
    log_a = (-LRU_C * jax.nn.softplus(-lam.astype(f32))) * r.astype(f32)
    a = jnp.exp(log_a)
    u = jnp.sqrt(-jnp.expm1(2.0 * log_a)) * (ig * xc).astype(f32)
    h, h_last = linear_scan(a, u, h0.astype(f32))
    y_r = h.astype(xn.dtype) * jax.nn.gelu(g_r)
    glu = a_c * jax.nn.sigmoid(b_c)
    yc, new_cconv = causal_dwconv(glu, cconv_buf, dw_w, dw_b)
    y_c = jax.nn.silu(layernorm(yc, ln_g, ln_b))
    y = jnp.einsum('bte,ed->btd', jnp.concatenate([y_r, y_c], axis=-1), w_out)
    return y, new_rconv, h_last, new_cconv


def mlstm_chunk_step(carry, xs):
    C, n, m = carry
    q, k, v, ig, lf = xs
    L = q.shape[2]
    b = lax.cumsum(lf, axis=2)
    causal = jnp.tril(jnp.ones((L, L), dtype=bool))
    dmat = jnp.where(causal, b[..., :, None] - b[..., None, :] + ig[..., None, :], -jnp.inf)
    inter = b + m[..., None]
    m_t = jnp.maximum(inter, jnp.max(dmat, axis=-1))
    s = jnp.einsum('bhtd,bhsd->bhts', q, k) * jnp.exp(dmat - m_t[..., None])
    scale_in = jnp.exp(inter - m_t)
    num = jnp.einsum('bhts,bhsv->bhtv', s, v) + scale_in[..., None] * jnp.einsum('bhtd,bhdv->bhtv', q, C)
    den = jnp.sum(s, axis=-1) + scale_in * jnp.einsum('bhtd,bhd->bht', q, n)
    h = num / jnp.maximum(jnp.abs(den), jnp.exp(-m_t))[..., None]
    b_last = b[..., -1]
    g = b_last[..., None] - b + ig
    m_new = jnp.maximum(b_last + m, jnp.max(g, axis=-1))
    decay = jnp.exp(b_last + m - m_new)
    wk = jnp.exp(g - m_new[..., None])
    C_new = decay[..., None, None] * C + jnp.einsum('bhs,bhsd,bhsv->bhdv', wk, k, v)
    n_new = decay[..., None] * n + jnp.einsum('bhs,bhsd->bhd', wk, k)
    return (C_new, n_new, m_new), h


def mlstm_mixer(xn, w_in, b_if, hn_g, w_out, C0, n0, m0):
    B, T, _ = xn.shape
    f32 = jnp.float32
    proj = jnp.einsum('btd,de->bte', xn, w_in)
    q, k, v, o, gif = jnp.split(proj, [ML_QK, 2 * ML_QK, 2 * ML_QK + ML_V, 2 * ML_QK + 2 * ML_V], axis=-1)

    def heads(z, dh):
        return z.reshape(B, T, ML_HEADS, dh).transpose(0, 2, 1, 3).astype(f32)

    q = heads(q, ML_DK) * (ML_DK ** -0.5)
    k = heads(k, ML_DK)
    v = heads(v, ML_DV)
    gif = (gif + b_if).astype(f32).transpose(0, 2, 1)
    ig = gif[:, :ML_HEADS]
    lf = jax.nn.log_sigmoid(gif[:, ML_HEADS:])
    L = ML_CHUNK if T % ML_CHUNK == 0 else T
    nc = T // L

    def chunks(z):
        return jnp.moveaxis(z.reshape(z.shape[:2] + (nc, L) + z.shape[3:]), 2, 0)

    (C, n, m), h = lax.scan(mlstm_chunk_step, (C0.astype(f32), n0.astype(f32), m0.astype(f32)),
                            (chunks(q), chunks(k), chunks(v), chunks(ig), chunks(lf)))
    h = jnp.moveaxis(h, 0, 2).reshape(B, ML_HEADS, T, ML_DV)
    h = h * lax.rsqrt(jnp.mean(h * h, axis=-1, keepdims=True) + EPS)
    h = h.transpose(0, 2, 1, 3).reshape(B, T, ML_V).astype(xn.dtype)
    y = (h * hn_g) * jax.nn.sigmoid(o)
    return jnp.einsum('bte,ed->btd', y, w_out), C, n, m


def sq_relu_mlp(xn, w_up, w_down):
    hdn = jnp.square(jax.nn.relu(jnp.einsum('btd,df->btf', xn, w_up)))
    return jnp.einsum('btf,fd->btd', hdn, w_down)


def run_trunk(x, rconv, rh, cconv, mC, mn, mm, P):
    o_rconv, o_rh, o_cconv, o_C, o_n, o_m = [], [], [], [], [], []
    for l in range(DEPTH):
        i = l // 2
        if l % 2 == 0:
            y, s1, s2, s3 = rglru_conformer_mixer(
                rmsnorm(x, P['ab_norm'][i]), P['ab_w_in'][i], P['rglru_conv_w'][i], P['rglru_conv_b'][i],
                P['rglru_wa'][i], P['rglru_ba'][i], P['rglru_wx'][i], P['rglru_bx'][i], P['rglru_lambda'][i],
                P['conf_dw_w'][i], P['conf_dw_b'][i], P['conf_ln_g'][i], P['conf_ln_b'][i], P['ab_w_out'][i],
                rconv[i], rh[i], cconv[i])
            o_rconv.append(s1); o_rh.append(s2); o_cconv.append(s3)
        else:
            y, C, n, m = mlstm_mixer(rmsnorm(x, P['ml_norm'][i]), P['ml_w_in'][i], P['ml_b_if'][i],
                                     P['ml_hnorm_g'][i], P['ml_w_out'][i], mC[i], mn[i], mm[i])
            o_C.append(C); o_n.append(n); o_m.append(m)
        x = x + y
        x = x + sq_relu_mlp(rmsnorm(x, P['mlp_norm'][l]), P['mlp_w_up'][l], P['mlp_w_down'][l])
    y_out = rmsnorm(x, P['final_norm'])
    return (y_out, jnp.stack(o_rconv), jnp.stack(o_rh), jnp.stack(o_cconv),
            jnp.stack(o_C), jnp.stack(o_n), jnp.stack(o_m))


def setup_inputs(seed: int = 0) -> dict:
    key = jax.random.key(seed)
    ks = jax.random.split(key, 40)
    f32 = jnp.float32
    nrm = lambda k, s, sc: jax.random.normal(k, s, f32) * sc
    u = jax.random.uniform(ks[0], (N_AB, D_RNN), f32, 0.9, 0.999)
    a0 = u ** (1.0 / LRU_C)
    lam = jnp.log(a0) - jnp.log1p(-a0)
    b_if = jnp.concatenate([nrm(ks[1], (N_ML, ML_HEADS), 0.1),
                            3.0 + nrm(ks[2], (N_ML, ML_HEADS), 0.5)], axis=-1)
    ml_in = 2 * ML_QK + 2 * ML_V + 2 * ML_HEADS
    return {
        'x_prompt': nrm(ks[3], (BATCH, SEQ, D_MODEL), 1.0),
        'x_sample': nrm(ks[4], (DEC_BATCH, DEC_SEQ, D_MODEL), 1.0),
        'state_rglru_conv': nrm(ks[5], (N_AB, DEC_BATCH, RNN_CONV_W - 1, D_RNN), 1.0),
        'state_rglru_h': nrm(ks[6], (N_AB, DEC_BATCH, D_RNN), 0.5),
        'state_conf_conv': nrm(ks[7], (N_AB, DEC_BATCH, CONF_CONV_W - 1, D_CONV), 0.5),
        'state_mlstm_C': nrm(ks[8], (N_ML, DEC_BATCH, ML_HEADS, ML_DK, ML_DV), 0.3),
        'state_mlstm_n': nrm(ks[9], (N_ML, DEC_BATCH, ML_HEADS, ML_DK), 0.3),
        'state_mlstm_m': nrm(ks[10], (N_ML, DEC_BATCH, ML_HEADS), 1.0),
        'ab_norm': 1.0 + nrm(ks[11], (N_AB, D_MODEL), 0.05),
        'ab_w_in': nrm(ks[12], (N_AB, D_MODEL, 2 * D_RNN + 2 * D_CONV), D_MODEL ** -0.5),
        'rglru_conv_w': nrm(ks[13], (N_AB, RNN_CONV_W, D_RNN), RNN_CONV_W ** -0.5),
        'rglru_conv_b': nrm(ks[14], (N_AB, D_RNN), 0.01),
        'rglru_wa': nrm(ks[15], (N_AB, RNN_HEADS, RNN_HEAD_DIM, RNN_HEAD_DIM), RNN_HEAD_DIM ** -0.5),
        'rglru_ba': nrm(ks[16], (N_AB, D_RNN), 0.01),
        'rglru_wx': nrm(ks[17], (N_AB, RNN_HEADS, RNN_HEAD_DIM, RNN_HEAD_DIM), RNN_HEAD_DIM ** -0.5),
        'rglru_bx': nrm(ks[18], (N_AB, D_RNN), 0.01),
        'rglru_lambda': lam,
        'conf_dw_w': nrm(ks[19], (N_AB, CONF_CONV_W, D_CONV), CONF_CONV_W ** -0.5),
        'conf_dw_b': nrm(ks[20], (N_AB, D_CONV), 0.01),
        'conf_ln_g': 1.0 + nrm(ks[21], (N_AB, D_CONV), 0.05),
        'conf_ln_b': nrm(ks[22], (N_AB, D_CONV), 0.01),
        'ab_w_out': nrm(ks[23], (N_AB, D_RNN + D_CONV, D_MODEL), (D_RNN + D_CONV) ** -0.5),
        'ml_norm': 1.0 + nrm(ks[24], (N_ML, D_MODEL), 0.05),
        'ml_w_in': nrm(ks[25], (N_ML, D_MODEL, ml_in), D_MODEL ** -0.5),
        'ml_b_if': b_if,
        'ml_hnorm_g': 1.0 + nrm(ks[26], (N_ML, ML_V), 0.05),
        'ml_w_out': nrm(ks[27], (N_ML, ML_V, D_MODEL), ML_V ** -0.5),
        'mlp_norm': 1.0 + nrm(ks[28], (DEPTH, D_MODEL), 0.05),
        'mlp_w_up': nrm(ks[29], (DEPTH, D_MODEL, D_FF), D_MODEL ** -0.5),
        'mlp_w_down': nrm(ks[30], (DEPTH, D_FF, D_MODEL), D_FF ** -0.5),
        'final_norm': 1.0 + nrm(ks[31], (D_MODEL,), 0.05),
    }


def reference(x_prompt, x_sample, state_rglru_conv, state_rglru_h, state_conf_conv, state_mlstm_C,
              state_mlstm_n, state_mlstm_m, ab_norm, ab_w_in, rglru_conv_w, rglru_conv_b, rglru_wa,
              rglru_ba, rglru_wx, rglru_bx, rglru_lambda, conf_dw_w, conf_dw_b, conf_ln_g, conf_ln_b,
              ab_w_out, ml_norm, ml_w_in, ml_b_if, ml_hnorm_g, ml_w_out, mlp_norm, mlp_w_up, mlp_w_down,
              final_norm):
    P = {'ab_norm': ab_norm, 'ab_w_in': ab_w_in, 'rglru_conv_w': rglru_conv_w, 'rglru_conv_b': rglru_conv_b,
         'rglru_wa': rglru_wa, 'rglru_ba': rglru_ba, 'rglru_wx': rglru_wx, 'rglru_bx': rglru_bx,
         'rglru_lambda': rglru_lambda, 'conf_dw_w': conf_dw_w, 'conf_dw_b': conf_dw_b,
         'conf_ln_g': conf_ln_g, 'conf_ln_b': conf_ln_b, 'ab_w_out': ab_w_out, 'ml_norm': ml_norm,
         'ml_w_in': ml_w_in, 'ml_b_if': ml_b_if, 'ml_hnorm_g': ml_hnorm_g, 'ml_w_out': ml_w_out,
         'mlp_norm': mlp_norm, 'mlp_w_up': mlp_w_up, 'mlp_w_down': mlp_w_down, 'final_norm': final_norm}
    f32 = jnp.float32
    Bp = x_prompt.shape[0]
    z_rconv = jnp.zeros((N_AB, Bp, RNN_CONV_W - 1, D_RNN), x_prompt.dtype)
    z_rh = jnp.zeros((N_AB, Bp, D_RNN), f32)
    z_cconv = jnp.zeros((N_AB, Bp, CONF_CONV_W - 1, D_CONV), x_prompt.dtype)
    z_C = jnp.zeros((N_ML, Bp, ML_HEADS, ML_DK, ML_DV), f32)
    z_n = jnp.zeros((N_ML, Bp, ML_HEADS, ML_DK), f32)
    z_m = jnp.zeros((N_ML, Bp, ML_HEADS), f32)
    y_prompt, p_rconv, p_rh, p_cconv, p_C, p_n, p_m = run_trunk(x_prompt, z_rconv, z_rh, z_cconv, z_C, z_n, z_m, P)
    y_sample, s_rconv, s_rh, s_cconv, s_C, s_n, s_m = run_trunk(
        x_sample, state_rglru_conv, state_rglru_h, state_conf_conv, state_mlstm_C, state_mlstm_n,
        state_mlstm_m, P)
    return (y_prompt, y_sample, p_rconv, p_rh, p_cconv, p_C, p_n, p_m,
            s_rconv, s_rh, s_cconv, s_C, s_n, s_m)
```

```python
import functools
import math

import jax
import jax.numpy as jnp
from jax import lax
from jax.experimental import pallas as pl
from jax.experimental.pallas import tpu as pltpu

F32 = jnp.float32
BF16 = jnp.bfloat16
EPS = 1e-6
LRU_C = 8.0
ML_CHUNK = 128
GATE_BLOCK = 256
SUBLANES = 8
MIB = 1024 * 1024


def _cp(sem, vmem_mib):
    return pltpu.CompilerParams(dimension_semantics=sem, vmem_limit_bytes=vmem_mib * MIB)


def _tile(n, pref):
    return pref if n % pref == 0 else n


def _rms(x, g):
    return (x * lax.rsqrt(jnp.mean(x * x, axis=-1, keepdims=True) + EPS)) * g


def _sigmoid(x):
    return jax.nn.sigmoid(x)


def _softplus(z):
    return jnp.maximum(z, 0.0) + jnp.log1p(jnp.exp(-jnp.abs(z)))


def _log_sigmoid(x):
    return -_softplus(-x)


def _gelu_tanh(x):
    c = math.sqrt(2.0 / math.pi)
    return x * (0.5 * (1.0 + jnp.tanh(c * (x + 0.044715 * (x * x * x)))))


def _layernorm(x, g, b):
    mu = jnp.mean(x, axis=-1, keepdims=True)
    d = x - mu
    var = jnp.mean(d * d, axis=-1, keepdims=True)
    return (d * lax.rsqrt(var + EPS)) * g + b


def _dot(a, b):
    return jnp.dot(a, b, preferred_element_type=F32)


def _split3(x):
    hi = x.astype(BF16)
    r1 = x - hi.astype(F32)
    mid = r1.astype(BF16)
    lo = (r1 - mid.astype(F32)).astype(BF16)
    return hi, mid, lo


def _rmsnorm_kernel(x_ref, g_ref, o_ref):
    o_ref[...] = _rms(x_ref[...], g_ref[...]).astype(o_ref.dtype)


def rmsnorm(x, g, out_dtype):
    M, D = x.shape
    tm = _tile(M, 512)
    return pl.pallas_call(
        _rmsnorm_kernel, grid=(M // tm,),
        in_specs=[pl.BlockSpec((tm, D), lambda i: (i, 0)), pl.BlockSpec((1, D), lambda i: (0, 0))],
        out_specs=pl.BlockSpec((tm, D), lambda i: (i, 0)),
        out_shape=jax.ShapeDtypeStruct((M, D), out_dtype),
        compiler_params=_cp(("parallel",), 32), name="rmsnorm")(x, g.reshape(1, D))


def _matmul_kernel(x_ref, w_ref, o_ref):
    o_ref[...] = _dot(x_ref[...], w_ref[...]).astype(o_ref.dtype)


def matmul(x, w, n_cols, tn):
    M, K = x.shape
    tm = _tile(M, 1024)
    tn = _tile(n_cols, tn)
    return pl.pallas_call(
        _matmul_kernel, grid=(n_cols // tn, M // tm),
        in_specs=[pl.BlockSpec((tm, K), lambda j, i: (i, 0)),
                  pl.BlockSpec((K, tn), lambda j, i: (0, j))],
        out_specs=pl.BlockSpec((tm, tn), lambda j, i: (i, j)),
        out_shape=jax.ShapeDtypeStruct((M, n_cols), F32),
        compiler_params=_cp(("parallel", "parallel"), 48), name="in_proj")(x, w)


def _outproj_kernel(*refs, n_act):
    acts = refs[:n_act]
    w_ref, x_ref, g_ref, xo_ref, xn_ref = refs[n_act:]
    y = None
    k0 = 0
    for a_ref in acts:
        kw = a_ref.shape[1]
        part = _dot(a_ref[...].astype(BF16), w_ref[k0:k0 + kw, :])
        y = part if y is None else y + part
        k0 += kw
    x_new = x_ref[...] + y
    xo_ref[...] = x_new
    xn_ref[...] = _rms(x_new, g_ref[...]).astype(xn_ref.dtype)


def out_proj(acts, w, x, g):
    M, D = x.shape
    K = w.shape[0]
    tm = _tile(M, 512)
    row = lambda i: (i, 0)
    fixed = lambda i: (0, 0)
    in_specs = [pl.BlockSpec((tm, a.shape[1]), row) for a in acts]
    in_specs += [pl.BlockSpec((K, D), fixed), pl.BlockSpec((tm, D), row), pl.BlockSpec((1, D), fixed)]
    return pl.pallas_call(
        functools.partial(_outproj_kernel, n_act=len(acts)), grid=(M // tm,),
        in_specs=in_specs,
        out_specs=[pl.BlockSpec((tm, D), row), pl.BlockSpec((tm, D), row)],
        out_shape=[jax.ShapeDtypeStruct((M, D), F32), jax.ShapeDtypeStruct((M, D), BF16)],
        compiler_params=_cp(("parallel",), 48), name="out_proj")(*acts, w, x, g.reshape(1, D))


def _mlp_kernel(xn_ref, x_ref, wu_ref, wd_ref, g_ref, xo_ref, xn_out_ref):
    f = pl.program_id(1)
    h = _dot(xn_ref[...], wu_ref[...])
    h = jnp.square(jnp.maximum(h, 0.0)).astype(BF16)
    part = _dot(h, wd_ref[...])

    @pl.when(f == 0)
    def _():
        xo_ref[...] = x_ref[...] + part

    @pl.when(f > 0)
    def _():
        xo_ref[...] += part

    @pl.when(f == pl.num_programs(1) - 1)
    def _():
        xn_out_ref[...] = _rms(xo_ref[...], g_ref[...]).astype(xn_out_ref.dtype)


def mlp(xn, x, w_up, w_down, g, out_dtype):
    M, D = x.shape
    F = w_up.shape[1]
    tm = _tile(M, 512)
    tf = _tile(F, 512)
    row = lambda i, f: (i, 0)
    return pl.pallas_call(
        _mlp_kernel, grid=(M // tm, F // tf),
        in_specs=[pl.BlockSpec((tm, D), row), pl.BlockSpec((tm, D), row),
                  pl.BlockSpec((D, tf), lambda i, f: (0, f)), pl.BlockSpec((tf, D), lambda i, f: (f, 0)),
                  pl.BlockSpec((1, D), lambda i, f: (0, 0))],
        out_specs=[pl.BlockSpec((tm, D), row), pl.BlockSpec((tm, D), row)],
        out_shape=[jax.ShapeDtypeStruct((M, D), F32), jax.ShapeDtypeStruct((M, D), out_dtype)],
        compiler_params=_cp(("parallel", "arbitrary"), 52), name="mlp")(xn, x, w_up, w_down, g.reshape(1, D))


def _rglru_gates(xc, g_r, wg_ref, ba, bx, lam):
    C = xc.shape[1]
    xcb = xc.astype(BF16)
    r_parts, i_parts = [], []
    for c in range(C // GATE_BLOCK):
        lo, hi = c * GATE_BLOCK, (c + 1) * GATE_BLOCK
        gates = _dot(xcb[:, lo:hi], wg_ref[c])
        r_parts.append(gates[:, :GATE_BLOCK])
        i_parts.append(gates[:, GATE_BLOCK:])
    r = _sigmoid(jnp.concatenate(r_parts, axis=1) + ba)
    ig = _sigmoid(jnp.concatenate(i_parts, axis=1) + bx)
    log_a = (-LRU_C * _softplus(-lam)) * r
    a = jnp.exp(log_a)
    u = jnp.sqrt(-jnp.tanh(log_a) * (a * a + 1.0)) * (ig * xc)
    return a, u


def _rglru_prompt_kernel(xr_ref, gr_ref, cw_ref, cb_ref, wg_ref, ba_ref, bx_ref, lam_ref,
                         y_ref, tail_ref, hlast_ref, xp_ref, a_ref, u_ref, h_ref):
    t = pl.program_id(1)
    tT, C = xr_ref.shape
    nk = cw_ref.shape[0]

    @pl.when(t == 0)
    def _():
        xp_ref[0:SUBLANES, :] = jnp.zeros((SUBLANES, C), F32)
        h_ref[...] = jnp.zeros_like(h_ref)

    x = xr_ref[...]
    xp_ref[SUBLANES:SUBLANES + tT, :] = x
    xc = cb_ref[...] + cw_ref[nk - 1:nk, :] * x
    for k in range(nk - 1):
        off = SUBLANES - (nk - 1) + k
        xc = xc + cw_ref[k:k + 1, :] * xp_ref[off:off + tT, :]
    xp_ref[0:SUBLANES, :] = x[tT - SUBLANES:tT, :]

    a, u = _rglru_gates(xc, gr_ref[...], wg_ref, ba_ref[...], bx_ref[...], lam_ref[...])
    a_ref[...] = a
    u_ref[...] = u

    rowi = lax.broadcasted_iota(jnp.int32, (SUBLANES, C), 0)

    def body(g, h):
        r0 = pl.multiple_of(g * SUBLANES, SUBLANES)
        a8 = a_ref[pl.ds(r0, SUBLANES), :]
        u8 = u_ref[pl.ds(r0, SUBLANES), :]
        for s in (1, 2, 4):
            a_sh = pltpu.roll(a8, s, 0)
            u_sh = pltpu.roll(u8, s, 0)
            m = rowi >= s
            u8 = jnp.where(m, a8 * u_sh + u8, u8)
            a8 = jnp.where(m, a8 * a_sh, a8)
        h8 = a8 * h + u8
        u_ref[pl.ds(r0, SUBLANES), :] = h8
        return h8[SUBLANES - 1:SUBLANES, :]

    h = lax.fori_loop(0, tT // SUBLANES, body, h_ref[...])
    h_ref[...] = h
    y_ref[...] = (u_ref[...] * _gelu_tanh(gr_ref[...])).astype(y_ref.dtype)

    @pl.when(t == pl.num_programs(1) - 1)
    def _():
        tail_ref[0] = x[tT - SUBLANES:tT, :]
        hlast_ref[0] = h


def rglru_prompt(proj, B, T, cw, cb, wg, ba, bx, lam):
    C = cw.shape[1]
    tT = _tile(T, 256)
    nT = T // tT
    fixed2 = lambda b, t: (0, 0)
    vec = pl.BlockSpec((1, C), fixed2)
    return pl.pallas_call(
        _rglru_prompt_kernel, grid=(B, nT),
        in_specs=[pl.BlockSpec((tT, C), lambda b, t: (b * nT + t, 0)),
                  pl.BlockSpec((tT, C), lambda b, t: (b * nT + t, 1)),
                  pl.BlockSpec(cw.shape, fixed2), vec,
                  pl.BlockSpec(wg.shape, lambda b, t: (0, 0, 0)), vec, vec, vec],
        out_specs=[pl.BlockSpec((tT, C), lambda b, t: (b * nT + t, 0)),
                   pl.BlockSpec((1, SUBLANES, C), lambda b, t: (b, 0, 0)),
                   pl.BlockSpec((1, 1, C), lambda b, t: (b, 0, 0))],
        out_shape=[jax.ShapeDtypeStruct((B * T, C), BF16),
                   jax.ShapeDtypeStruct((B, SUBLANES, C), F32),
                   jax.ShapeDtypeStruct((B, 1, C), F32)],
        scratch_shapes=[pltpu.VMEM((SUBLANES + tT, C), F32), pltpu.VMEM((tT, C), F32),
                        pltpu.VMEM((tT, C), F32), pltpu.VMEM((1, C), F32)],
        compiler_params=_cp(("parallel", "arbitrary"), 40), name="rglru_prompt",
    )(proj, proj, cw, cb.reshape(1, C), wg, ba.reshape(1, C), bx.reshape(1, C), lam.reshape(1, C))


CONF_ROWS = 64
LANES = 128


def _conf_prompt_kernel(ac_ref, bc_ref, dw_ref, db_ref, lg_ref, lb_ref,
                        y_ref, tail_ref, gp_ref, yc_ref):
    t = pl.program_id(1)
    tT, C = ac_ref.shape
    nk = dw_ref.shape[0]
    pad = gp_ref.shape[0] - tT

    @pl.when(t == 0)
    def _():
        gp_ref[0:pad, :] = jnp.zeros((pad, C), F32)

    gp_ref[pad:pad + tT, :] = ac_ref[...] * _sigmoid(bc_ref[...])
    rb = min(CONF_ROWS, tT)
    for c0 in range(0, C, LANES):
        for r0 in range(0, tT, rb):
            acc = jnp.broadcast_to(db_ref[:, c0:c0 + LANES], (rb, LANES))
            for k in range(nk):
                off = pad - (nk - 1) + k + r0
                acc = acc + gp_ref[off:off + rb, c0:c0 + LANES] * dw_ref[k:k + 1, c0:c0 + LANES]
            yc_ref[r0:r0 + rb, c0:c0 + LANES] = acc
    yn = _layernorm(yc_ref[...], lg_ref[...], lb_ref[...])
    y_ref[...] = (yn * _sigmoid(yn)).astype(y_ref.dtype)

    @pl.when(t == pl.num_programs(1) - 1)
    def _():
        tail_ref[0] = gp_ref[tT:tT + pad, :]

    gp_ref[0:pad, :] = gp_ref[tT:tT + pad, :]


def conf_prompt(proj, B, T, dw, db, lg, lb):
    nk, C = dw.shape
    tT = _tile(T, 128)
    nT = T // tT
    pad = -(-(nk - 1) // SUBLANES) * SUBLANES
    fixed2 = lambda b, t: (0, 0)
    vec = pl.BlockSpec((1, C), fixed2)
    return pl.pallas_call(
        _conf_prompt_kernel, grid=(B, nT),
        in_specs=[pl.BlockSpec((tT, C), lambda b, t: (b * nT + t, 2)),
                  pl.BlockSpec((tT, C), lambda b, t: (b * nT + t, 3)),
                  pl.BlockSpec((nk, C), fixed2), vec, vec, vec],
        out_specs=[pl.BlockSpec((tT, C), lambda b, t: (b * nT + t, 0)),
                   pl.BlockSpec((1, pad, C), lambda b, t: (b, 0, 0))],
        out_shape=[jax.ShapeDtypeStruct((B * T, C), BF16), jax.ShapeDtypeStruct((B, pad, C), F32)],
        scratch_shapes=[pltpu.VMEM((pad + tT, C), F32), pltpu.VMEM((tT, C), F32)],
        compiler_params=_cp(("parallel", "arbitrary"), 32), name="conf_prompt",
    )(proj, proj, dw, db.reshape(1, C), lg.reshape(1, C), lb.reshape(1, C))


def _ab_sample_kernel(proj_ref, rconv_ref, h0_ref, cconv_ref,
                      cw_ref, cb_ref, wg_ref, ba_ref, bx_ref, lam_ref, dw_ref, db_ref, lg_ref, lb_ref,
                      yr_ref, yc_ref, rconv_o, h_o, cconv_o, xc_s, ycv_s):
    bt = proj_ref.shape[0]
    C = h0_ref.shape[1]
    nk = cw_ref.shape[0]
    nd = dw_ref.shape[0]
    x_r = proj_ref[:, 0:C]
    g_r = proj_ref[:, C:2 * C]
    glu = proj_ref[:, 2 * C:3 * C] * _sigmoid(proj_ref[:, 3 * C:4 * C])
    for b in range(bt):
        xb = x_r[b:b + 1, :]
        buf = rconv_ref[b]
        xc = cb_ref[...] + cw_ref[nk - 1:nk, :] * xb
        xc = xc + jnp.sum(buf * cw_ref[0:nk - 1, :], axis=0, keepdims=True)
        xc_s[b:b + 1, :] = xc
        rconv_o[b, 0:nk - 2, :] = buf[1:nk - 1, :]
        rconv_o[b, nk - 2:nk - 1, :] = xb
        gb = glu[b:b + 1, :]
        cbuf = cconv_ref[b]
        yc = db_ref[...] + dw_ref[nd - 1:nd, :] * gb
        yc = yc + jnp.sum(cbuf * dw_ref[0:nd - 1, :], axis=0, keepdims=True)
        ycv_s[b:b + 1, :] = yc
        cconv_o[b, 0:nd - 2, :] = cbuf[1:nd - 1, :]
        cconv_o[b, nd - 2:nd - 1, :] = gb
    a, u = _rglru_gates(xc_s[...], g_r, wg_ref, ba_ref[...], bx_ref[...], lam_ref[...])
    h = a * h0_ref[...] + u
    h_o[...] = h
    yr_ref[...] = h * _gelu_tanh(g_r)
    yn = _layernorm(ycv_s[...], lg_ref[...], lb_ref[...])
    yc_ref[...] = yn * _sigmoid(yn)


def ab_sample(proj, rconv, h0, cconv, cw, cb, wg, ba, bx, lam, dw, db, lg, lb):
    B, C = h0.shape
    bt = _tile(B, SUBLANES)
    row = lambda i: (i, 0)
    row3 = lambda i: (i, 0, 0)
    fixed = lambda i: (0, 0)
    vec = pl.BlockSpec((1, C), fixed)
    st_r = pl.BlockSpec((bt,) + rconv.shape[1:], row3)
    st_c = pl.BlockSpec((bt,) + cconv.shape[1:], row3)
    act = pl.BlockSpec((bt, C), row)
    return pl.pallas_call(
        _ab_sample_kernel, grid=(B // bt,),
        in_specs=[pl.BlockSpec((bt, proj.shape[1]), row), st_r, act, st_c,
                  pl.BlockSpec(cw.shape, fixed), vec, pl.BlockSpec(wg.shape, lambda i: (0, 0, 0)),
                  vec, vec, vec, pl.BlockSpec(dw.shape, fixed), vec, vec, vec],
        out_specs=[act, act, st_r, act, st_c],
        out_shape=[jax.ShapeDtypeStruct((B, C), F32), jax.ShapeDtypeStruct((B, C), F32),
                   jax.ShapeDtypeStruct(rconv.shape, F32), jax.ShapeDtypeStruct((B, C), F32),
                   jax.ShapeDtypeStruct(cconv.shape, F32)],
        scratch_shapes=[pltpu.VMEM((bt, C), F32), pltpu.VMEM((bt, C), F32)],
        compiler_params=_cp(("parallel",), 32), name="ab_sample",
    )(proj, rconv, h0, cconv, cw, cb.reshape(1, C), wg, ba.reshape(1, C), bx.reshape(1, C),
      lam.reshape(1, C), dw, db.reshape(1, C), lg.reshape(1, C), lb.reshape(1, C))


def _mlstm_prompt_kernel(q_ref, k_ref, v_ref, o_ref, g_ref, bif_ref, hng_ref,
                         y_ref, C_ref, n_ref, m_ref, *, heads):
    c = pl.program_id(1)
    L = q_ref.shape[0]
    H = heads
    DK = q_ref.shape[1] // H
    DV = v_ref.shape[1] // H
    scale = DK ** -0.5

    @pl.when(c == 0)
    def _():
        C_ref[...] = jnp.zeros_like(C_ref)
        n_ref[...] = jnp.zeros_like(n_ref)
        m_ref[...] = jnp.zeros_like(m_ref)

    ti = lax.broadcasted_iota(jnp.int32, (L, L), 0)
    si = lax.broadcasted_iota(jnp.int32, (L, L), 1)
    causal = si <= ti
    tri_lo = causal.astype(BF16)
    tri_up = (ti <= si).astype(BF16)

    gates = g_ref[...] + bif_ref[...]
    lf = _log_sigmoid(gates)
    gates_t = gates.T
    lf_t = lf.T
    hi, mid, lo = _split3(lf)
    b_cols = (_dot(tri_lo, lo) + _dot(tri_lo, mid)) + _dot(tri_lo, hi)
    hi, mid, lo = _split3(lf_t)
    b_rows = (_dot(lo, tri_up) + _dot(mid, tri_up)) + _dot(hi, tri_up)

    for h in range(H):
        qs = q_ref[:, h * DK:(h + 1) * DK] * scale
        qb = qs.astype(BF16)
        kf = k_ref[:, h * DK:(h + 1) * DK]
        kb = kf.astype(BF16)
        vb = v_ref[:, h * DV:(h + 1) * DV].astype(BF16)
        ig_col = gates[:, h:h + 1]
        ig_row = gates_t[h:h + 1, :]
        b_col = b_cols[:, H + h:H + h + 1]
        b_row = b_rows[H + h:H + h + 1, :]
        m_prev = m_ref[0, h:h + 1, 0:1]
        n_row = n_ref[0, h:h + 1, :]
        Ch = C_ref[0, h]

        dmat = jnp.where(causal, b_col - b_row + ig_row, -jnp.inf)
        inter = b_col + m_prev
        m_t = jnp.maximum(inter, jnp.max(dmat, axis=-1, keepdims=True))
        qk = lax.dot_general(qb, kb, (((1,), (1,)), ((), ())), preferred_element_type=F32)
        s = qk * jnp.exp(dmat - m_t)
        scale_in = jnp.exp(inter - m_t)
        num = _dot(s.astype(BF16), vb) + scale_in * _dot(qb, Ch.astype(BF16))
        den = jnp.sum(s, axis=-1, keepdims=True) + scale_in * jnp.sum(qs * n_row, axis=-1, keepdims=True)
        hh = num / jnp.maximum(jnp.abs(den), jnp.exp(-m_t))

        b_last = b_col[L - 1:L, :]
        g_row = b_last - b_row + ig_row
        g_col = b_last - b_col + ig_col
        m_new = jnp.maximum(b_last + m_prev, jnp.max(g_row, axis=-1, keepdims=True))
        decay = jnp.exp(b_last + m_prev - m_new)
        kw = kf * jnp.exp(g_col - m_new)
        C_ref[0, h] = decay * Ch + lax.dot_general(kw.astype(BF16), vb, (((0,), (0,)), ((), ())),
                                                   preferred_element_type=F32)
        n_ref[0, h:h + 1, :] = decay * n_row + jnp.sum(kw, axis=0, keepdims=True)
        m_ref[0, h:h + 1, :] = jnp.broadcast_to(m_new, (1, m_ref.shape[2]))

        hn = hh * lax.rsqrt(jnp.mean(hh * hh, axis=-1, keepdims=True) + EPS)
        y = (hn * hng_ref[:, h * DV:(h + 1) * DV]) * _sigmoid(o_ref[:, h * DV:(h + 1) * DV])
        y_ref[:, h * DV:(h + 1) * DV] = y.astype(y_ref.dtype)


def mlstm_prompt(proj, gates, bif_row, hng, B, T, heads):
    QK = proj.shape[1] // 6
    V = 2 * QK
    H = heads
    DK, DV = QK // H, V // H
    L = ML_CHUNK if T % ML_CHUNK == 0 else T
    nc = T // L
    rowmap = lambda j: (lambda b, c: (b * nc + c, j))
    fixed2 = lambda b, c: (0, 0)
    return pl.pallas_call(
        functools.partial(_mlstm_prompt_kernel, heads=H), grid=(B, nc),
        in_specs=[pl.BlockSpec((L, QK), rowmap(0)), pl.BlockSpec((L, QK), rowmap(1)),
                  pl.BlockSpec((L, V), rowmap(1)), pl.BlockSpec((L, V), rowmap(2)),
                  pl.BlockSpec((L, LANES), rowmap(0)), pl.BlockSpec((1, LANES), fixed2),
                  pl.BlockSpec((1, V), fixed2)],
        out_specs=[pl.BlockSpec((L, V), rowmap(0)),
                   pl.BlockSpec((1, H, DK, DV), lambda b, c: (b, 0, 0, 0)),
                   pl.BlockSpec((1, H, DK), lambda b, c: (b, 0, 0)),
                   pl.BlockSpec((1, H, LANES), lambda b, c: (b, 0, 0))],
        out_shape=[jax.ShapeDtypeStruct((B * T, V), BF16),
                   jax.ShapeDtypeStruct((B, H, DK, DV), F32),
                   jax.ShapeDtypeStruct((B, H, DK), F32),
                   jax.ShapeDtypeStruct((B, H, LANES), F32)],
        compiler_params=_cp(("parallel", "arbitrary"), 40), name="mlstm_prompt",
    )(proj, proj, proj, proj, gates, bif_row, hng.reshape(1, V))


def _mlstm_gate_terms(ig, fg, m):
    lf = _log_sigmoid(fg)
    inter = lf + m
    m_t = jnp.maximum(inter, ig)
    return m_t, jnp.exp(ig - m_t), jnp.exp(inter - m_t)


def _mlstm_sample_kernel(qt_ref, kt_ref, qr_ref, kr_ref, v_ref, o_ref, gr_ref, gc_ref, bifr_ref, bifc_ref,
                         hng_ref, C_ref, n_ref, y_ref, Co_ref, no_ref, mo_ref):
    H, DK = qr_ref.shape[1], qr_ref.shape[2]
    scale = DK ** -0.5
    gr = gr_ref[0]
    gc = gc_ref[0]
    bifr = bifr_ref[...]
    bifc = bifc_ref[...]
    mt_r, es_r, sc_r = _mlstm_gate_terms(gr[:, 0:H] + bifr[:, 0:H], gr[:, H:2 * H] + bifr[:, H:2 * H],
                                         gr[:, 2 * H:3 * H])
    mt_c, es_c, sc_c = _mlstm_gate_terms(gc[0:H, :] + bifc[0:H, :], gc[H:2 * H, :] + bifc[H:2 * H, :],
                                         gc[2 * H:3 * H, :])
    qr = qr_ref[0] * scale
    kr = kr_ref[0]
    n = n_ref[0]
    v = v_ref[0]
    qk = jnp.sum(qr * kr, axis=1, keepdims=True)
    qn = jnp.sum(qr * n, axis=1, keepdims=True)
    qt = qt_ref[0] * scale
    kts = kt_ref[0] * es_r
    rows = []
    for h in range(H):
        Ch = C_ref[0, h]
        rows.append(jnp.sum(Ch * qt[:, h:h + 1], axis=0, keepdims=True))
        Co_ref[0, h] = sc_r[:, h:h + 1] * Ch + kts[:, h:h + 1] * v[h:h + 1, :]
    qC = jnp.concatenate(rows, axis=0)
    s = qk * es_c
    num = s * v + sc_c * qC
    den = s + sc_c * qn
    hh = num / jnp.maximum(jnp.abs(den), jnp.exp(-mt_c))
    hn = hh * lax.rsqrt(jnp.mean(hh * hh, axis=-1, keepdims=True) + EPS)
    y_ref[0] = (hn * hng_ref[...]) * _sigmoid(o_ref[0])
    no_ref[0] = sc_c * n + es_c * kr
    mo_ref[0] = mt_r


def mlstm_sample(q, k, v, o, gates, m, bif, hng, C, n):
    B, H, DK = q.shape
    DV = v.shape[2]
    gm = jnp.concatenate([gates, m], axis=1)
    qt = jnp.swapaxes(q, 1, 2)
    kt = jnp.swapaxes(k, 1, 2)
    b3 = lambda i: (i, 0, 0)
    fixed = lambda i: (0, 0)
    blk = lambda a: pl.BlockSpec((1,) + a.shape[1:], b3)
    C_spec = pl.BlockSpec((1, H, DK, DV), lambda i: (i, 0, 0, 0))
    return pl.pallas_call(
        _mlstm_sample_kernel, grid=(B,),
        in_specs=[blk(qt), blk(kt), blk(q), blk(k), blk(v), blk(o),
                  pl.BlockSpec((1, 1, 3 * H), b3), pl.BlockSpec((1, 3 * H, 1), b3),
                  pl.BlockSpec((1, 2 * H), fixed), pl.BlockSpec((2 * H, 1), fixed),
                  pl.BlockSpec((H, DV), fixed), C_spec, blk(n)],
        out_specs=[blk(v), C_spec, blk(n), pl.BlockSpec((1, 1, H), b3)],
        out_shape=[jax.ShapeDtypeStruct((B, H, DV), F32), jax.ShapeDtypeStruct(C.shape, F32),
                   jax.ShapeDtypeStruct(n.shape, F32), jax.ShapeDtypeStruct((B, 1, H), F32)],
        compiler_params=_cp(("parallel",), 32), name="mlstm_sample",
    )(qt, kt, q, k, v, o, gm.reshape(B, 1, 3 * H), gm.reshape(B, 3 * H, 1),
      bif.reshape(1, 2 * H), bif.reshape(2 * H, 1), hng.reshape(H, DV), C, n)


def _blockdiag(w, per):
    H, d, _ = w.shape
    w4 = w.reshape(H // per, per, d, d)
    eye = jnp.eye(per, dtype=w.dtype)
    return (w4[:, :, :, None, :] * eye[None, :, None, :, None]).reshape(H // per, per * d, per * d)


def _prep_weights(P):
    W = {}
    W['ab_w_in'] = P['ab_w_in'].astype(BF16)
    W['ab_w_out'] = P['ab_w_out'].astype(BF16)
    W['ml_w_out'] = P['ml_w_out'].astype(BF16)
    W['mlp_w_up'] = P['mlp_w_up'].astype(BF16)
    W['mlp_w_down'] = P['mlp_w_down'].astype(BF16)
    heads = P['ml_b_if'].shape[1] // 2
    n_main = P['ml_w_in'].shape[2] - 2 * heads
    W['ml_w_main'] = P['ml_w_in'][:, :, :n_main].astype(BF16)
    W['ml_w_gate'] = jnp.pad(P['ml_w_in'][:, :, n_main:], ((0, 0), (0, 0), (0, LANES - 2 * heads))).astype(BF16)
    W['ml_bif_row'] = jnp.pad(P['ml_b_if'], ((0, 0), (0, LANES - 2 * heads)))[:, None, :]
    d = P['rglru_wa'].shape[2]
    per = GATE_BLOCK // d
    wa = jax.vmap(lambda w: _blockdiag(w, per))(P['rglru_wa'])
    wx = jax.vmap(lambda w: _blockdiag(w, per))(P['rglru_wx'])
    W['rglru_wg'] = jnp.concatenate([wa, wx], axis=-1).astype(BF16)
    return W


def _run_trunk(x2d, P, W, B, T, states):
    depth = P['mlp_norm'].shape[0]
    D = x2d.shape[1]
    heads = P['ml_b_if'].shape[1] // 2
    o_rconv, o_rh, o_cconv, o_C, o_n, o_m = [], [], [], [], [], []
    x = x2d
    xn = rmsnorm(x, P['ab_norm'][0], BF16)
    for l in range(depth):
        i = l // 2
        if l % 2 == 0:
            proj = matmul(xn, W['ab_w_in'][i], W['ab_w_in'].shape[2], 1024)
            rg = (P['rglru_conv_w'][i], P['rglru_conv_b'][i], W['rglru_wg'][i], P['rglru_ba'][i],
                  P['rglru_bx'][i], P['rglru_lambda'][i])
            cf = (P['conf_dw_w'][i], P['conf_dw_b'][i], P['conf_ln_g'][i], P['conf_ln_b'][i])
            if states is None:
                y_r, tail_r, h_last = rglru_prompt(proj, B, T, *rg)
                y_c, tail_c = conf_prompt(proj, B, T, *cf)
                nr = P['rglru_conv_w'].shape[1] - 1
                ncf = P['conf_dw_w'].shape[1] - 1
                o_rconv.append(tail_r[:, tail_r.shape[1] - nr:])
                o_rh.append(h_last[:, 0])
                o_cconv.append(tail_c[:, tail_c.shape[1] - ncf:])
            else:
                y_r, y_c, s_rconv, s_h, s_cconv = ab_sample(proj, states[0][i], states[1][i], states[2][i],
                                                            *rg, *cf)
                o_rconv.append(s_rconv); o_rh.append(s_h); o_cconv.append(s_cconv)
            x, xn = out_proj([y_r, y_c], W['ab_w_out'][i], x, P['mlp_norm'][l])
        else:
            n_main = W['ml_w_main'].shape[2]
            proj = matmul(xn, W['ml_w_main'][i], n_main, 1024)
            gates = matmul(xn, W['ml_w_gate'][i], LANES, LANES)
            hng = P['ml_hnorm_g'][i]
            if states is None:
                y, C, n, m = mlstm_prompt(proj, gates, W['ml_bif_row'][i], hng, B, T, heads)
                m = m[:, :, 0]
            else:
                QK = n_main // 6
                V = 2 * QK
                q = proj[:, :QK].reshape(B, heads, QK // heads)
                k = proj[:, QK:2 * QK].reshape(B, heads, QK // heads)
                v = proj[:, 2 * QK:2 * QK + V].reshape(B, heads, V // heads)
                o = proj[:, 2 * QK + V:].reshape(B, heads, V // heads)
                y, C, n, m = mlstm_sample(q, k, v, o, gates[:, :2 * heads], states[5][i], P['ml_b_if'][i],
                                          hng, states[3][i], states[4][i])
                y = y.reshape(B, V)
                m = m[:, 0]
            o_C.append(C); o_n.append(n); o_m.append(m)
            x, xn = out_proj([y], W['ml_w_out'][i], x, P['mlp_norm'][l])
        last = l == depth - 1
        if last:
            g_next = P['final_norm']
        elif (l + 1) % 2 == 0:
            g_next = P['ab_norm'][(l + 1) // 2]
        else:
            g_next = P['ml_norm'][(l + 1) // 2]
        x, xn = mlp(xn, x, W['mlp_w_up'][l], W['mlp_w_down'][l], g_next, F32 if last else BF16)
    return (xn, jnp.stack(o_rconv), jnp.stack(o_rh), jnp.stack(o_cconv),
            jnp.stack(o_C), jnp.stack(o_n), jnp.stack(o_m))


def kernel(x_prompt, x_sample, state_rglru_conv, state_rglru_h, state_conf_conv, state_mlstm_C, state_mlstm_n, state_mlstm_m, ab_norm, ab_w_in, rglru_conv_w, rglru_conv_b, rglru_wa, rglru_ba, rglru_wx, rglru_bx, rglru_lambda, conf_dw_w, conf_dw_b, conf_ln_g, conf_ln_b, ab_w_out, ml_norm, ml_w_in, ml_b_if, ml_hnorm_g, ml_w_out, mlp_norm, mlp_w_up, mlp_w_down, final_norm):
    P = {'ab_norm': ab_norm, 'ab_w_in': ab_w_in, 'rglru_conv_w': rglru_conv_w, 'rglru_conv_b': rglru_conv_b,
         'rglru_wa': rglru_wa, 'rglru_ba': rglru_ba, 'rglru_wx': rglru_wx, 'rglru_bx': rglru_bx,
         'rglru_lambda': rglru_lambda, 'conf_dw_w': conf_dw_w, 'conf_dw_b': conf_dw_b,
         'conf_ln_g': conf_ln_g, 'conf_ln_b': conf_ln_b, 'ab_w_out': ab_w_out, 'ml_norm': ml_norm,
         'ml_w_in': ml_w_in, 'ml_b_if': ml_b_if, 'ml_hnorm_g': ml_hnorm_g, 'ml_w_out': ml_w_out,
         'mlp_norm': mlp_norm, 'mlp_w_up': mlp_w_up, 'mlp_w_down': mlp_w_down, 'final_norm': final_norm}
    W = _prep_weights(P)
    Bp, Tp, D = x_prompt.shape
    Bs, Ts, _ = x_sample.shape
    assert Ts == 1, "the sample group advances one token per sequence"
    yp, *p_states = _run_trunk(x_prompt.reshape(Bp * Tp, D), P, W, Bp, Tp, None)
    ys, *s_states = _run_trunk(x_sample.reshape(Bs, D), P, W, Bs, 1,
                               (state_rglru_conv, state_rglru_h, state_conf_conv,
                                state_mlstm_C, state_mlstm_n, state_mlstm_m))
    return (yp.reshape(Bp, Tp, D), ys.reshape(Bs, Ts, D), *p_states, *s_states)
```

```python
import functools
import math

import jax
import jax.numpy as jnp
from jax import lax
from jax.experimental import pallas as pl
from jax.experimental.pallas import tpu as pltpu

F32 = jnp.float32
BF16 = jnp.bfloat16
EPS = 1e-6
LRU_C = 8.0
ML_CHUNK = 128
GATE_BLOCK = 256
SUBLANES = 8
MIB = 1024 * 1024


def _cp(sem, vmem_mib):
    return pltpu.CompilerParams(dimension_semantics=sem, vmem_limit_bytes=vmem_mib * MIB)


def _tile(n, pref):
    return pref if n % pref == 0 else n


def _rms(x, g):
    return (x * lax.rsqrt(jnp.mean(x * x, axis=-1, keepdims=True) + EPS)) * g


def _sigmoid(x):
    return jax.nn.sigmoid(x)


def _softplus(z):
    return jnp.maximum(z, 0.0) + jnp.log1p(jnp.exp(-jnp.abs(z)))


def _log_sigmoid(x):
    return -_softplus(-x)


def _gelu_tanh(x):
    c = math.sqrt(2.0 / math.pi)
    return x * (0.5 * (1.0 + jnp.tanh(c * (x + 0.044715 * (x * x * x)))))


def _layernorm(x, g, b):
    mu = jnp.mean(x, axis=-1, keepdims=True)
    d = x - mu
    var = jnp.mean(d * d, axis=-1, keepdims=True)
    return (d * lax.rsqrt(var + EPS)) * g + b


def _dot(a, b):
    return jnp.dot(a, b, preferred_element_type=F32)


def _split3(x):
    hi = x.astype(BF16)
    r1 = x - hi.astype(F32)
    mid = r1.astype(BF16)
    lo = (r1 - mid.astype(F32)).astype(BF16)
    return hi, mid, lo


def _rmsnorm_kernel(x_ref, g_ref, o_ref):
    o_ref[...] = _rms(x_ref[...], g_ref[...]).astype(o_ref.dtype)


def rmsnorm(x, g, out_dtype):
    M, D = x.shape
    tm = _tile(M, 512)
    return pl.pallas_call(
        _rmsnorm_kernel, grid=(M // tm,),
        in_specs=[pl.BlockSpec((tm, D), lambda i: (i, 0)), pl.BlockSpec((1, D), lambda i: (0, 0))],
        out_specs=pl.BlockSpec((tm, D), lambda i: (i, 0)),
        out_shape=jax.ShapeDtypeStruct((M, D), out_dtype),
        compiler_params=_cp(("parallel",), 32), name="rmsnorm")(x, g.reshape(1, D))


def _matmul_kernel(x_ref, w_ref, o_ref, wb_ref):
    @pl.when(pl.program_id(1) == 0)
    def _():
        wb_ref[...] = w_ref[...].astype(BF16)

    o_ref[...] = _dot(x_ref[...], wb_ref[...]).astype(o_ref.dtype)


def matmul(x, w, layer, n_cols, tn):
    M, K = x.shape
    tm = _tile(M, 1024)
    tn = _tile(n_cols, tn)
    return pl.pallas_call(
        _matmul_kernel, grid=(n_cols // tn, M // tm),
        in_specs=[pl.BlockSpec((tm, K), lambda j, i: (i, 0)),
                  pl.BlockSpec((None, K, tn), lambda j, i: (layer, 0, j))],
        out_specs=pl.BlockSpec((tm, tn), lambda j, i: (i, j)),
        out_shape=jax.ShapeDtypeStruct((M, n_cols), F32),
        scratch_shapes=[pltpu.VMEM((K, tn), BF16)],
        compiler_params=_cp(("parallel", "arbitrary"), 48), name="in_proj")(x, w)


def _outproj_kernel(*refs, n_act):
    acts = refs[:n_act]
    w_ref, x_ref, g_ref, xo_ref, xn_ref = refs[n_act:]
    y = None
    k0 = 0
    for a_ref in acts:
        kw = a_ref.shape[1]
        part = _dot(a_ref[...].astype(BF16), w_ref[k0:k0 + kw, :])
        y = part if y is None else y + part
        k0 += kw
    x_new = x_ref[...] + y
    xo_ref[...] = x_new
    xn_ref[...] = _rms(x_new, g_ref[...]).astype(xn_ref.dtype)


def out_proj(acts, w, layer, x, g):
    M, D = x.shape
    K = w.shape[1]
    tm = _tile(M, 512)
    row = lambda i: (i, 0)
    fixed = lambda i: (0, 0)
    in_specs = [pl.BlockSpec((tm, a.shape[1]), row) for a in acts]
    in_specs += [pl.BlockSpec((None, K, D), lambda i: (layer, 0, 0), pipeline_mode=pl.Buffered(1)),
                 pl.BlockSpec((tm, D), row), pl.BlockSpec((1, D), fixed)]
    return pl.pallas_call(
        functools.partial(_outproj_kernel, n_act=len(acts)), grid=(M // tm,),
        in_specs=in_specs,
        out_specs=[pl.BlockSpec((tm, D), row), pl.BlockSpec((tm, D), row)],
        out_shape=[jax.ShapeDtypeStruct((M, D), F32), jax.ShapeDtypeStruct((M, D), BF16)],
        compiler_params=_cp(("parallel",), 48), name="out_proj")(*acts, w, x, g.reshape(1, D))


def _mlp_kernel(xn_ref, x_ref, wu_ref, wd_ref, g_ref, xo_ref, xn_out_ref):
    f = pl.program_id(1)

    @pl.when(f == 0)
    def _():
        xo_ref[...] = x_ref[...]

    h = _dot(xn_ref[...], wu_ref[...])
    h = jnp.square(jnp.maximum(h, 0.0)).astype(BF16)
    xo_ref[...] += _dot(h, wd_ref[...])

    @pl.when(f == pl.num_programs(1) - 1)
    def _():
        xn_out_ref[...] = _rms(xo_ref[...], g_ref[...]).astype(xn_out_ref.dtype)


def mlp(xn, x, w_up, w_down, layer, g, out_dtype):
    M, D = x.shape
    F = w_up.shape[2]
    tm = _tile(M, 1024)
    tf = _tile(F, 512)
    row = lambda i, f: (i, 0)
    once = pl.Buffered(1)
    return pl.pallas_call(
        _mlp_kernel, grid=(M // tm, F // tf),
        in_specs=[pl.BlockSpec((tm, D), row), pl.BlockSpec((tm, D), row, pipeline_mode=once),
                  pl.BlockSpec((None, D, tf), lambda i, f: (layer, 0, f)),
                  pl.BlockSpec((None, tf, D), lambda i, f: (layer, f, 0)),
                  pl.BlockSpec((1, D), lambda i, f: (0, 0))],
        out_specs=[pl.BlockSpec((tm, D), row, pipeline_mode=once),
                   pl.BlockSpec((tm, D), row, pipeline_mode=once)],
        out_shape=[jax.ShapeDtypeStruct((M, D), F32), jax.ShapeDtypeStruct((M, D), out_dtype)],
        compiler_params=_cp(("parallel", "arbitrary"), 56), name="mlp")(xn, x, w_up, w_down, g.reshape(1, D))


def _rglru_gates(xc, g_r, wg_ref, ba, bx, lam):
    C = xc.shape[1]
    xcb = xc.astype(BF16)
    r_parts, i_parts = [], []
    for c in range(C // GATE_BLOCK):
        lo, hi = c * GATE_BLOCK, (c + 1) * GATE_BLOCK
        gates = _dot(xcb[:, lo:hi], wg_ref[c])
        r_parts.append(gates[:, :GATE_BLOCK])
        i_parts.append(gates[:, GATE_BLOCK:])
    r = _sigmoid(jnp.concatenate(r_parts, axis=1) + ba)
    ig = _sigmoid(jnp.concatenate(i_parts, axis=1) + bx)
    log_a = (-LRU_C * _softplus(-lam)) * r
    a = jnp.exp(log_a)
    u = jnp.sqrt(-jnp.tanh(log_a) * (a * a + 1.0)) * (ig * xc)
    return a, u


def _rglru_prompt_kernel(xr_ref, gr_ref, cw_ref, cb_ref, wg_ref, ba_ref, bx_ref, lam_ref,
                         y_ref, tail_ref, hlast_ref, xp_ref, a_ref, u_ref, h_ref):
    t = pl.program_id(1)
    tT, C = xr_ref.shape
    nk = cw_ref.shape[0]

    @pl.when(t == 0)
    def _():
        xp_ref[0:SUBLANES, :] = jnp.zeros((SUBLANES, C), F32)
        h_ref[...] = jnp.zeros_like(h_ref)

    x = xr_ref[...]
    xp_ref[SUBLANES:SUBLANES + tT, :] = x
    xc = cb_ref[...] + cw_ref[nk - 1:nk, :] * x
    for k in range(nk - 1):
        off = SUBLANES - (nk - 1) + k
        xc = xc + cw_ref[k:k + 1, :] * xp_ref[off:off + tT, :]
    xp_ref[0:SUBLANES, :] = x[tT - SUBLANES:tT, :]

    a, u = _rglru_gates(xc, gr_ref[...], wg_ref, ba_ref[...], bx_ref[...], lam_ref[...])
    a_ref[...] = a
    u_ref[...] = u

    rowi = lax.broadcasted_iota(jnp.int32, (SUBLANES, C), 0)

    def body(g, h):
        r0 = pl.multiple_of(g * SUBLANES, SUBLANES)
        a8 = a_ref[pl.ds(r0, SUBLANES), :]
        u8 = u_ref[pl.ds(r0, SUBLANES), :]
        for s in (1, 2, 4):
            a_sh = pltpu.roll(a8, s, 0)
            u_sh = pltpu.roll(u8, s, 0)
            m = rowi >= s
            u8 = jnp.where(m, a8 * u_sh + u8, u8)
            a8 = jnp.where(m, a8 * a_sh, a8)
        h8 = a8 * h + u8
        u_ref[pl.ds(r0, SUBLANES), :] = h8
        return h8[SUBLANES - 1:SUBLANES, :]

    h = lax.fori_loop(0, tT // SUBLANES, body, h_ref[...])
    h_ref[...] = h
    y_ref[...] = (u_ref[...] * _gelu_tanh(gr_ref[...])).astype(y_ref.dtype)

    @pl.when(t == pl.num_programs(1) - 1)
    def _():
        tail_ref[0] = x[tT - SUBLANES:tT, :]
        hlast_ref[0] = h


def rglru_prompt(proj, B, T, cw, cb, wg, ba, bx, lam):
    C = cw.shape[1]
    tT = _tile(T, 256)
    nT = T // tT
    fixed2 = lambda b, t: (0, 0)
    vec = pl.BlockSpec((1, C), fixed2)
    return pl.pallas_call(
        _rglru_prompt_kernel, grid=(B, nT),
        in_specs=[pl.BlockSpec((tT, C), lambda b, t: (b * nT + t, 0)),
                  pl.BlockSpec((tT, C), lambda b, t: (b * nT + t, 1)),
                  pl.BlockSpec(cw.shape, fixed2), vec,
                  pl.BlockSpec(wg.shape, lambda b, t: (0, 0, 0)), vec, vec, vec],
        out_specs=[pl.BlockSpec((tT, C), lambda b, t: (b * nT + t, 0)),
                   pl.BlockSpec((1, SUBLANES, C), lambda b, t: (b, 0, 0)),
                   pl.BlockSpec((1, 1, C), lambda b, t: (b, 0, 0))],
        out_shape=[jax.ShapeDtypeStruct((B * T, C), BF16),
                   jax.ShapeDtypeStruct((B, SUBLANES, C), F32),
                   jax.ShapeDtypeStruct((B, 1, C), F32)],
        scratch_shapes=[pltpu.VMEM((SUBLANES + tT, C), F32), pltpu.VMEM((tT, C), F32),
                        pltpu.VMEM((tT, C), F32), pltpu.VMEM((1, C), F32)],
        compiler_params=_cp(("parallel", "arbitrary"), 40), name="rglru_prompt",
    )(proj, proj, cw, cb.reshape(1, C), wg, ba.reshape(1, C), bx.reshape(1, C), lam.reshape(1, C))


CONF_ROWS = 64
LANES = 128


def _conf_prompt_kernel(ac_ref, bc_ref, dw_ref, db_ref, lg_ref, lb_ref,
                        y_ref, tail_ref, gp_ref, yc_ref):
    t = pl.program_id(1)
    tT, C = ac_ref.shape
    nk = dw_ref.shape[0]
    pad = gp_ref.shape[0] - tT

    @pl.when(t == 0)
    def _():
        gp_ref[0:pad, :] = jnp.zeros((pad, C), F32)

    gp_ref[pad:pad + tT, :] = ac_ref[...] * _sigmoid(bc_ref[...])
    rb = min(CONF_ROWS, tT)
    for c0 in range(0, C, LANES):
        for r0 in range(0, tT, rb):
            acc = jnp.broadcast_to(db_ref[:, c0:c0 + LANES], (rb, LANES))
            for k in range(nk):
                off = pad - (nk - 1) + k + r0
                acc = acc + gp_ref[off:off + rb, c0:c0 + LANES] * dw_ref[k:k + 1, c0:c0 + LANES]
            yc_ref[r0:r0 + rb, c0:c0 + LANES] = acc
    yn = _layernorm(yc_ref[...], lg_ref[...], lb_ref[...])
    y_ref[...] = (yn * _sigmoid(yn)).astype(y_ref.dtype)

    @pl.when(t == pl.num_programs(1) - 1)
    def _():
        tail_ref[0] = gp_ref[tT:tT + pad, :]

    gp_ref[0:pad, :] = gp_ref[tT:tT + pad, :]


def conf_prompt(proj, B, T, dw, db, lg, lb):
    nk, C = dw.shape
    tT = _tile(T, 128)
    nT = T // tT
    pad = -(-(nk - 1) // SUBLANES) * SUBLANES
    fixed2 = lambda b, t: (0, 0)
    vec = pl.BlockSpec((1, C), fixed2)
    return pl.pallas_call(
        _conf_prompt_kernel, grid=(B, nT),
        in_specs=[pl.BlockSpec((tT, C), lambda b, t: (b * nT + t, 2)),
                  pl.BlockSpec((tT, C), lambda b, t: (b * nT + t, 3)),
                  pl.BlockSpec((nk, C), fixed2), vec, vec, vec],
        out_specs=[pl.BlockSpec((tT, C), lambda b, t: (b * nT + t, 0)),
                   pl.BlockSpec((1, pad, C), lambda b, t: (b, 0, 0))],
        out_shape=[jax.ShapeDtypeStruct((B * T, C), BF16), jax.ShapeDtypeStruct((B, pad, C), F32)],
        scratch_shapes=[pltpu.VMEM((pad + tT, C), F32), pltpu.VMEM((tT, C), F32)],
        compiler_params=_cp(("parallel", "arbitrary"), 32), name="conf_prompt",
    )(proj, proj, dw, db.reshape(1, C), lg.reshape(1, C), lb.reshape(1, C))


def _ab_sample_kernel(*refs, has_acc):
    (proj_ref, rconv_ref, h0_ref, cconv_ref,
     cw_ref, cb_ref, wg_ref, ba_ref, bx_ref, lam_ref, dw_ref, db_ref, lg_ref, lb_ref) = refs[:14]
    yr_ref, yc_ref, rconv_o, h_o, cconv_o, xc_s, ycv_s = refs[14 + int(has_acc):]
    bt = proj_ref.shape[0]
    C = h0_ref.shape[1]
    nk = cw_ref.shape[0]
    nd = dw_ref.shape[0]
    x_r = proj_ref[:, 0:C]
    g_r = proj_ref[:, C:2 * C]
    glu = proj_ref[:, 2 * C:3 * C] * _sigmoid(proj_ref[:, 3 * C:4 * C])
    for b in range(bt):
        xb = x_r[b:b + 1, :]
        buf = rconv_ref[b]
        xc = cb_ref[...] + cw_ref[nk - 1:nk, :] * xb
        xc = xc + jnp.sum(buf * cw_ref[0:nk - 1, :], axis=0, keepdims=True)
        xc_s[b:b + 1, :] = xc
        rconv_o[b, 0:nk - 2, :] = buf[1:nk - 1, :]
        rconv_o[b, nk - 2:nk - 1, :] = xb
        gb = glu[b:b + 1, :]
        cbuf = cconv_ref[b]
        yc = db_ref[...] + dw_ref[nd - 1:nd, :] * gb
        yc = yc + jnp.sum(cbuf * dw_ref[0:nd - 1, :], axis=0, keepdims=True)
        ycv_s[b:b + 1, :] = yc
        cconv_o[b, 0:nd - 2, :] = cbuf[1:nd - 1, :]
        cconv_o[b, nd - 2:nd - 1, :] = gb
    a, u = _rglru_gates(xc_s[...], g_r, wg_ref, ba_ref[...], bx_ref[...], lam_ref[...])
    h = a * h0_ref[...] + u
    h_o[...] = h
    yr_ref[...] = h * _gelu_tanh(g_r)
    yn = _layernorm(ycv_s[...], lg_ref[...], lb_ref[...])
    yc_ref[...] = yn * _sigmoid(yn)


def ab_sample(proj, rconv, h0, cconv, layer, cconv_acc, cw, cb, wg, ba, bx, lam, dw, db, lg, lb):
    _, B, C = h0.shape
    bt = _tile(B, SUBLANES)
    row = lambda i: (i, 0)
    fixed = lambda i: (0, 0)
    vec = pl.BlockSpec((1, C), fixed)
    st_r_in = pl.BlockSpec((None, bt) + rconv.shape[2:], lambda i: (layer, i, 0, 0))
    st_c = pl.BlockSpec((None, bt) + cconv.shape[2:], lambda i: (layer, i, 0, 0))
    act = pl.BlockSpec((bt, C), row)
    has_acc = cconv_acc is not None
    in_specs = [pl.BlockSpec((bt, proj.shape[1]), row), st_r_in,
                pl.BlockSpec((None, bt, C), lambda i: (layer, i, 0)), st_c,
                pl.BlockSpec(cw.shape, fixed), vec, pl.BlockSpec(wg.shape, lambda i: (0, 0, 0)),
                vec, vec, vec, pl.BlockSpec(dw.shape, fixed), vec, vec, vec]
    args = [proj, rconv, h0, cconv, cw, cb.reshape(1, C), wg, ba.reshape(1, C), bx.reshape(1, C),
            lam.reshape(1, C), dw, db.reshape(1, C), lg.reshape(1, C), lb.reshape(1, C)]
    if has_acc:
        in_specs.append(pl.BlockSpec(memory_space=pl.ANY))
        args.append(cconv_acc)
    return pl.pallas_call(
        functools.partial(_ab_sample_kernel, has_acc=has_acc), grid=(B // bt,),
        in_specs=in_specs,
        out_specs=[act, act, pl.BlockSpec((bt,) + rconv.shape[2:], lambda i: (i, 0, 0)), act, st_c],
        out_shape=[jax.ShapeDtypeStruct((B, C), F32), jax.ShapeDtypeStruct((B, C), F32),
                   jax.ShapeDtypeStruct(rconv.shape[1:], F32), jax.ShapeDtypeStruct((B, C), F32),
                   jax.ShapeDtypeStruct(cconv.shape, F32)],
        scratch_shapes=[pltpu.VMEM((bt, C), F32), pltpu.VMEM((bt, C), F32)],
        input_output_aliases={len(args) - 1: 4} if has_acc else {},
        compiler_params=_cp(("parallel",), 32), name="ab_sample",
    )(*args)


def _mlstm_prompt_kernel(q_ref, k_ref, v_ref, o_ref, g_ref, bif_ref, hng_ref,
                         y_ref, C_ref, n_ref, m_ref, *, heads):
    c = pl.program_id(1)
    L = q_ref.shape[0]
    H = heads
    DK = q_ref.shape[1] // H
    DV = v_ref.shape[1] // H
    scale = DK ** -0.5

    @pl.when(c == 0)
    def _():
        C_ref[...] = jnp.zeros_like(C_ref)
        n_ref[...] = jnp.zeros_like(n_ref)
        m_ref[...] = jnp.zeros_like(m_ref)

    ti = lax.broadcasted_iota(jnp.int32, (L, L), 0)
    si = lax.broadcasted_iota(jnp.int32, (L, L), 1)
    causal = si <= ti
    tri_lo = causal.astype(BF16)
    tri_up = (ti <= si).astype(BF16)

    gates = g_ref[...] + bif_ref[...]
    lf = _log_sigmoid(gates)
    gates_t = gates.T
    lf_t = lf.T
    hi, mid, lo = _split3(lf)
    b_cols = (_dot(tri_lo, lo) + _dot(tri_lo, mid)) + _dot(tri_lo, hi)
    hi, mid, lo = _split3(lf_t)
    b_rows = (_dot(lo, tri_up) + _dot(mid, tri_up)) + _dot(hi, tri_up)

    for h in range(H):
        qs = q_ref[:, h * DK:(h + 1) * DK] * scale
        qb = qs.astype(BF16)
        kf = k_ref[:, h * DK:(h + 1) * DK]
        kb = kf.astype(BF16)
        vb = v_ref[:, h * DV:(h + 1) * DV].astype(BF16)
        ig_col = gates[:, h:h + 1]
        ig_row = gates_t[h:h + 1, :]
        b_col = b_cols[:, H + h:H + h + 1]
        b_row = b_rows[H + h:H + h + 1, :]
        m_prev = m_ref[0, h:h + 1, 0:1]
        n_row = n_ref[0, h:h + 1, :]
        Ch = C_ref[0, h]

        dmat = jnp.where(causal, b_col - b_row + ig_row, -jnp.inf)
        inter = b_col + m_prev
        m_t = jnp.maximum(inter, jnp.max(dmat, axis=-1, keepdims=True))
        qk = lax.dot_general(qb, kb, (((1,), (1,)), ((), ())), preferred_element_type=F32)
        s = qk * jnp.exp(dmat - m_t)
        scale_in = jnp.exp(inter - m_t)
        num = _dot(s.astype(BF16), vb) + scale_in * _dot(qb, Ch.astype(BF16))
        den = jnp.sum(s, axis=-1, keepdims=True) + scale_in * jnp.sum(qs * n_row, axis=-1, keepdims=True)
        hh = num / jnp.maximum(jnp.abs(den), jnp.exp(-m_t))

        b_last = b_col[L - 1:L, :]
        g_row = b_last - b_row + ig_row
        g_col = b_last - b_col + ig_col
        m_new = jnp.maximum(b_last + m_prev, jnp.max(g_row, axis=-1, keepdims=True))
        decay = jnp.exp(b_last + m_prev - m_new)
        kw = kf * jnp.exp(g_col - m_new)
        C_ref[0, h] = decay * Ch + lax.dot_general(kw.astype(BF16), vb, (((0,), (0,)), ((), ())),
                                                   preferred_element_type=F32)
        n_ref[0, h:h + 1, :] = decay * n_row + jnp.sum(kw, axis=0, keepdims=True)
        m_ref[0, h:h + 1, :] = jnp.broadcast_to(m_new, (1, m_ref.shape[2]))

        hn = hh * lax.rsqrt(jnp.mean(hh * hh, axis=-1, keepdims=True) + EPS)
        y = (hn * hng_ref[:, h * DV:(h + 1) * DV]) * _sigmoid(o_ref[:, h * DV:(h + 1) * DV])
        y_ref[:, h * DV:(h + 1) * DV] = y.astype(y_ref.dtype)


def mlstm_prompt(proj, gates, bif_row, hng, B, T, heads):
    QK = proj.shape[1] // 6
    V = 2 * QK
    H = heads
    DK, DV = QK // H, V // H
    L = ML_CHUNK if T % ML_CHUNK == 0 else T
    nc = T // L
    rowmap = lambda j: (lambda b, c: (b * nc + c, j))
    fixed2 = lambda b, c: (0, 0)
    return pl.pallas_call(
        functools.partial(_mlstm_prompt_kernel, heads=H), grid=(B, nc),
        in_specs=[pl.BlockSpec((L, QK), rowmap(0)), pl.BlockSpec((L, QK), rowmap(1)),
                  pl.BlockSpec((L, V), rowmap(1)), pl.BlockSpec((L, V), rowmap(2)),
                  pl.BlockSpec((L, LANES), rowmap(0)), pl.BlockSpec((1, LANES), fixed2),
                  pl.BlockSpec((1, V), fixed2)],
        out_specs=[pl.BlockSpec((L, V), rowmap(0)),
                   pl.BlockSpec((1, H, DK, DV), lambda b, c: (b, 0, 0, 0)),
                   pl.BlockSpec((1, H, DK), lambda b, c: (b, 0, 0)),
                   pl.BlockSpec((1, H, LANES), lambda b, c: (b, 0, 0))],
        out_shape=[jax.ShapeDtypeStruct((B * T, V), BF16),
                   jax.ShapeDtypeStruct((B, H, DK, DV), F32),
                   jax.ShapeDtypeStruct((B, H, DK), F32),
                   jax.ShapeDtypeStruct((B, H, LANES), F32)],
        compiler_params=_cp(("parallel", "arbitrary"), 40), name="mlstm_prompt",
    )(proj, proj, proj, proj, gates, bif_row, hng.reshape(1, V))


def _mlstm_gate_terms(ig, fg, m):
    lf = _log_sigmoid(fg)
    inter = lf + m
    m_t = jnp.maximum(inter, ig)
    return m_t, jnp.exp(ig - m_t), jnp.exp(inter - m_t)


def _mlstm_sample_kernel(*refs, has_acc):
    (qt_ref, kt_ref, qr_ref, kr_ref, v_ref, o_ref, gr_ref, gc_ref, bifr_ref, bifc_ref,
     hng_ref, C_ref, n_ref) = refs[:13]
    y_ref, Co_ref, no_ref, mo_ref = refs[13 + int(has_acc):]
    bb, H, DK = qr_ref.shape
    scale = DK ** -0.5
    bifr = bifr_ref[...]
    bifc = bifc_ref[...]
    for j in range(bb):
        gr = gr_ref[j]
        gc = gc_ref[j]
        mt_r, es_r, sc_r = _mlstm_gate_terms(gr[:, 0:H] + bifr[:, 0:H], gr[:, H:2 * H] + bifr[:, H:2 * H],
                                             gr[:, 2 * H:3 * H])
        mt_c, es_c, sc_c = _mlstm_gate_terms(gc[0:H, :] + bifc[0:H, :], gc[H:2 * H, :] + bifc[H:2 * H, :],
                                             gc[2 * H:3 * H, :])
        qr = qr_ref[j] * scale
        kr = kr_ref[j]
        n = n_ref[j]
        v = v_ref[j]
        qk = jnp.sum(qr * kr, axis=1, keepdims=True)
        qn = jnp.sum(qr * n, axis=1, keepdims=True)
        qt = qt_ref[j] * scale
        kts = kt_ref[j] * es_r
        rows = []
        for h in range(H):
            Ch = C_ref[j, h]
            rows.append(jnp.sum(Ch * qt[:, h:h + 1], axis=0, keepdims=True))
            Co_ref[j, h] = sc_r[:, h:h + 1] * Ch + kts[:, h:h + 1] * v[h:h + 1, :]
        qC = jnp.concatenate(rows, axis=0)
        s = qk * es_c
        num = s * v + sc_c * qC
        den = s + sc_c * qn
        hh = num / jnp.maximum(jnp.abs(den), jnp.exp(-mt_c))
        hn = hh * lax.rsqrt(jnp.mean(hh * hh, axis=-1, keepdims=True) + EPS)
        y_ref[j] = (hn * hng_ref[...]) * _sigmoid(o_ref[j])
        no_ref[j] = sc_c * n + es_c * kr
        mo_ref[j] = mt_r


def mlstm_sample(q, k, v, o, gates, m, bif, hng, C, n, layer, C_acc):
    B, H, DK = q.shape
    DV = v.shape[2]
    bb = _tile(B, 4)
    gm = jnp.concatenate([gates, m], axis=1)
    qt = jnp.swapaxes(q, 1, 2)
    kt = jnp.swapaxes(k, 1, 2)
    b3 = lambda i: (i, 0, 0)
    fixed = lambda i: (0, 0)
    blk = lambda a: pl.BlockSpec((bb,) + a.shape[1:], b3)
    C_spec = pl.BlockSpec((None, bb, H, DK, DV), lambda i: (layer, i, 0, 0, 0))
    has_acc = C_acc is not None
    in_specs = [blk(qt), blk(kt), blk(q), blk(k), blk(v), blk(o),
                pl.BlockSpec((bb, 1, 3 * H), b3), pl.BlockSpec((bb, 3 * H, 1), b3),
                pl.BlockSpec((1, 2 * H), fixed), pl.BlockSpec((2 * H, 1), fixed),
                pl.BlockSpec((H, DV), fixed), C_spec,
                pl.BlockSpec((None, bb, H, DK), lambda i: (layer, i, 0, 0))]
    args = [qt, kt, q, k, v, o, gm.reshape(B, 1, 3 * H), gm.reshape(B, 3 * H, 1),
            bif.reshape(1, 2 * H), bif.reshape(2 * H, 1), hng.reshape(H, DV), C, n]
    if has_acc:
        in_specs.append(pl.BlockSpec(memory_space=pl.ANY))
        args.append(C_acc)
    return pl.pallas_call(
        functools.partial(_mlstm_sample_kernel, has_acc=has_acc), grid=(B // bb,),
        in_specs=in_specs,
        out_specs=[blk(v), C_spec, pl.BlockSpec((bb, H, DK), b3), pl.BlockSpec((bb, 1, H), b3)],
        out_shape=[jax.ShapeDtypeStruct((B, H, DV), F32), jax.ShapeDtypeStruct(C.shape, F32),
                   jax.ShapeDtypeStruct((B, H, DK), F32), jax.ShapeDtypeStruct((B, 1, H), F32)],
        input_output_aliases={len(args) - 1: 1} if has_acc else {},
        compiler_params=_cp(("parallel",), 40), name="mlstm_sample",
    )(*args)


def _blockdiag(w, per):
    H, d, _ = w.shape
    w4 = w.reshape(H // per, per, d, d)
    eye = jnp.eye(per, dtype=w.dtype)
    return (w4[:, :, :, None, :] * eye[None, :, None, :, None]).reshape(H // per, per * d, per * d)


def _prep_weights(P):
    W = {}
    W['ab_w_out'] = P['ab_w_out'].astype(BF16)
    W['ml_w_out'] = P['ml_w_out'].astype(BF16)
    W['mlp_w_up'] = P['mlp_w_up'].astype(BF16)
    W['mlp_w_down'] = P['mlp_w_down'].astype(BF16)
    heads = P['ml_b_if'].shape[1] // 2
    n_main = P['ml_w_in'].shape[2] - 2 * heads
    W['ml_w_gate'] = jnp.pad(P['ml_w_in'][:, :, n_main:], ((0, 0), (0, 0), (0, LANES - 2 * heads)))
    W['ml_bif_row'] = jnp.pad(P['ml_b_if'], ((0, 0), (0, LANES - 2 * heads)))[:, None, :]
    d = P['rglru_wa'].shape[2]
    per = GATE_BLOCK // d
    wa = jax.vmap(lambda w: _blockdiag(w, per))(P['rglru_wa'])
    wx = jax.vmap(lambda w: _blockdiag(w, per))(P['rglru_wx'])
    W['rglru_wg'] = jnp.concatenate([wa, wx], axis=-1).astype(BF16)
    return W


def _run_trunk(x2d, P, W, B, T, states):
    depth = P['mlp_norm'].shape[0]
    heads = P['ml_b_if'].shape[1] // 2
    n_main = P['ml_w_in'].shape[2] - 2 * heads
    o_rconv, o_rh, o_cconv, o_C, o_n, o_m = [], [], [], [], [], []
    cconv_acc = None
    C_acc = None
    x = x2d
    xn = rmsnorm(x, P['ab_norm'][0], BF16)
    for l in range(depth):
        i = l // 2
        if l % 2 == 0:
            proj = matmul(xn, P['ab_w_in'], i, P['ab_w_in'].shape[2], 1024)
            rg = (P['rglru_conv_w'][i], P['rglru_conv_b'][i], W['rglru_wg'][i], P['rglru_ba'][i],
                  P['rglru_bx'][i], P['rglru_lambda'][i])
            cf = (P['conf_dw_w'][i], P['conf_dw_b'][i], P['conf_ln_g'][i], P['conf_ln_b'][i])
            if states is None:
                y_r, tail_r, h_last = rglru_prompt(proj, B, T, *rg)
                y_c, tail_c = conf_prompt(proj, B, T, *cf)
                nr = P['rglru_conv_w'].shape[1] - 1
                ncf = P['conf_dw_w'].shape[1] - 1
                o_rconv.append(tail_r[:, tail_r.shape[1] - nr:])
                o_rh.append(h_last[:, 0])
                o_cconv.append(tail_c[:, tail_c.shape[1] - ncf:])
            else:
                y_r, y_c, s_rconv, s_h, cconv_acc = ab_sample(proj, states[0], states[1], states[2], i,
                                                              cconv_acc, *rg, *cf)
                o_rconv.append(s_rconv); o_rh.append(s_h)
            x, xn = out_proj([y_r, y_c], W['ab_w_out'], i, x, P['mlp_norm'][l])
        else:
            proj = matmul(xn, P['ml_w_in'], i, n_main, 1024)
            gates = matmul(xn, W['ml_w_gate'], i, LANES, LANES)
            hng = P['ml_hnorm_g'][i]
            if states is None:
                y, C, n, m = mlstm_prompt(proj, gates, W['ml_bif_row'][i], hng, B, T, heads)
                o_C.append(C)
                m = m[:, :, 0]
            else:
                QK = n_main // 6
                V = 2 * QK
                q = proj[:, :QK].reshape(B, heads, QK // heads)
                k = proj[:, QK:2 * QK].reshape(B, heads, QK // heads)
                v = proj[:, 2 * QK:2 * QK + V].reshape(B, heads, V // heads)
                o = proj[:, 2 * QK + V:].reshape(B, heads, V // heads)
                y, C_acc, n, m = mlstm_sample(q, k, v, o, gates[:, :2 * heads], states[5][i], P['ml_b_if'][i],
                                              hng, states[3], states[4], i, C_acc)
                y = y.reshape(B, V)
                m = m[:, 0]
            o_n.append(n); o_m.append(m)
            x, xn = out_proj([y], W['ml_w_out'], i, x, P['mlp_norm'][l])
        last = l == depth - 1
        if last:
            g_next = P['final_norm']
        elif (l + 1) % 2 == 0:
            g_next = P['ab_norm'][(l + 1) // 2]
        else:
            g_next = P['ml_norm'][(l + 1) // 2]
        x, xn = mlp(xn, x, W['mlp_w_up'], W['mlp_w_down'], l, g_next, F32 if last else BF16)
    cconv_out = jnp.stack(o_cconv) if states is None else cconv_acc
    C_out = jnp.stack(o_C) if states is None else C_acc
    return (xn, jnp.stack(o_rconv), jnp.stack(o_rh), cconv_out, C_out, jnp.stack(o_n), jnp.stack(o_m))


def kernel(x_prompt, x_sample, state_rglru_conv, state_rglru_h, state_conf_conv, state_mlstm_C, state_mlstm_n, state_mlstm_m, ab_norm, ab_w_in, rglru_conv_w, rglru_conv_b, rglru_wa, rglru_ba, rglru_wx, rglru_bx, rglru_lambda, conf_dw_w, conf_dw_b, conf_ln_g, conf_ln_b, ab_w_out, ml_norm, ml_w_in, ml_b_if, ml_hnorm_g, ml_w_out, mlp_norm, mlp_w_up, mlp_w_down, final_norm):
    P = {'ab_norm': ab_norm, 'ab_w_in': ab_w_in, 'rglru_conv_w': rglru_conv_w, 'rglru_conv_b': rglru_conv_b,
         'rglru_wa': rglru_wa, 'rglru_ba': rglru_ba, 'rglru_wx': rglru_wx, 'rglru_bx': rglru_bx,
         'rglru_lambda': rglru_lambda, 'conf_dw_w': conf_dw_w, 'conf_dw_b': conf_dw_b,
         'conf_ln_g': conf_ln_g, 'conf_ln_b': conf_ln_b, 'ab_w_out': ab_w_out, 'ml_norm': ml_norm,
         'ml_w_in': ml_w_in, 'ml_b_if': ml_b_if, 'ml_hnorm_g': ml_hnorm_g, 'ml_w_out': ml_w_out,
         'mlp_norm': mlp_norm, 'mlp_w_up': mlp_w_up, 'mlp_w_down': mlp_w_down, 'final_norm': final_norm}
    W = _prep_weights(P)
    Bp, Tp, D = x_prompt.shape
    Bs, Ts, _ = x_sample.shape
    assert Ts == 1, "the sample group advances one token per sequence"
    yp, *p_states = _run_trunk(x_prompt.reshape(Bp * Tp, D), P, W, Bp, Tp, None)
    ys, *s_states = _run_trunk(x_sample.reshape(Bs, D), P, W, Bs, 1,
                               (state_rglru_conv, state_rglru_h, state_conf_conv,
                                state_mlstm_C, state_mlstm_n, state_mlstm_m))
    return (yp.reshape(Bp, Tp, D), ys.reshape(Bs, Ts, D), *p_states, *s_states)
```

```python
import functools
import math

import jax
import jax.numpy as jnp
from jax import lax
from jax.experimental import pallas as pl
from jax.experimental.pallas import tpu as pltpu

F32 = jnp.float32
BF16 = jnp.bfloat16
EPS = 1e-6
LRU_C = 8.0
ML_CHUNK = 128
GATE_BLOCK = 256
SUBLANES = 8
MIB = 1024 * 1024


def _cp(sem, vmem_mib):
    return pltpu.CompilerParams(dimension_semantics=sem, vmem_limit_bytes=vmem_mib * MIB)


def _tile(n, pref):
    return pref if n % pref == 0 else n


def _rms(x, g):
    return (x * lax.rsqrt(jnp.mean(x * x, axis=-1, keepdims=True) + EPS)) * g


def _sigmoid(x):
    return jax.nn.sigmoid(x)


def _softplus(z):
    return jnp.maximum(z, 0.0) + jnp.log1p(jnp.exp(-jnp.abs(z)))


def _log_sigmoid(x):
    return -_softplus(-x)


def _gelu_tanh(x):
    c = math.sqrt(2.0 / math.pi)
    return x * (0.5 * (1.0 + jnp.tanh(c * (x + 0.044715 * (x * x * x)))))


def _layernorm(x, g, b):
    mu = jnp.mean(x, axis=-1, keepdims=True)
    d = x - mu
    var = jnp.mean(d * d, axis=-1, keepdims=True)
    return (d * lax.rsqrt(var + EPS)) * g + b


def _dot(a, b):
    return jnp.dot(a, b, preferred_element_type=F32)


def _split3(x):
    hi = x.astype(BF16)
    r1 = x - hi.astype(F32)
    mid = r1.astype(BF16)
    lo = (r1 - mid.astype(F32)).astype(BF16)
    return hi, mid, lo


def _rmsnorm_kernel(x_ref, g_ref, o_ref):
    o_ref[...] = _rms(x_ref[...], g_ref[...]).astype(o_ref.dtype)


def rmsnorm(x, g, out_dtype):
    M, D = x.shape
    tm = _tile(M, 512)
    return pl.pallas_call(
        _rmsnorm_kernel, grid=(M // tm,),
        in_specs=[pl.BlockSpec((tm, D), lambda i: (i, 0)), pl.BlockSpec((1, D), lambda i: (0, 0))],
        out_specs=pl.BlockSpec((tm, D), lambda i: (i, 0)),
        out_shape=jax.ShapeDtypeStruct((M, D), out_dtype),
        compiler_params=_cp(("parallel",), 32), name="rmsnorm")(x, g.reshape(1, D))


def _matmul_kernel(x_ref, w_ref, o_ref, wb_ref):
    @pl.when(pl.program_id(1) == 0)
    def _():
        wb_ref[...] = w_ref[...].astype(BF16)

    o_ref[...] = _dot(x_ref[...], wb_ref[...]).astype(o_ref.dtype)


def matmul(x, w, layer, n_cols, tn):
    M, K = x.shape
    tm = _tile(M, 1024)
    tn = _tile(n_cols, tn)
    return pl.pallas_call(
        _matmul_kernel, grid=(n_cols // tn, M // tm),
        in_specs=[pl.BlockSpec((tm, K), lambda j, i: (i, 0)),
                  pl.BlockSpec((None, K, tn), lambda j, i: (layer, 0, j))],
        out_specs=pl.BlockSpec((tm, tn), lambda j, i: (i, j)),
        out_shape=jax.ShapeDtypeStruct((M, n_cols), F32),
        scratch_shapes=[pltpu.VMEM((K, tn), BF16)],
        compiler_params=_cp(("parallel", "arbitrary"), 48), name="in_proj")(x, w)


def _outproj_kernel(*refs, n_act):
    acts = refs[:n_act]
    w_ref, x_ref, g_ref, xo_ref, xn_ref = refs[n_act:]
    y = None
    k0 = 0
    for a_ref in acts:
        kw = a_ref.shape[1]
        part = _dot(a_ref[...].astype(BF16), w_ref[k0:k0 + kw, :])
        y = part if y is None else y + part
        k0 += kw
    x_new = x_ref[...] + y
    xo_ref[...] = x_new
    xn_ref[...] = _rms(x_new, g_ref[...]).astype(xn_ref.dtype)


def out_proj(acts, w, layer, x, g):
    M, D = x.shape
    K = w.shape[1]
    tm = _tile(M, 512)
    row = lambda i: (i, 0)
    fixed = lambda i: (0, 0)
    in_specs = [pl.BlockSpec((tm, a.shape[1]), row) for a in acts]
    in_specs += [pl.BlockSpec((None, K, D), lambda i: (layer, 0, 0), pipeline_mode=pl.Buffered(1)),
                 pl.BlockSpec((tm, D), row), pl.BlockSpec((1, D), fixed)]
    return pl.pallas_call(
        functools.partial(_outproj_kernel, n_act=len(acts)), grid=(M // tm,),
        in_specs=in_specs,
        out_specs=[pl.BlockSpec((tm, D), row), pl.BlockSpec((tm, D), row)],
        out_shape=[jax.ShapeDtypeStruct((M, D), F32), jax.ShapeDtypeStruct((M, D), BF16)],
        compiler_params=_cp(("parallel",), 48), name="out_proj")(*acts, w, x, g.reshape(1, D))


def _mlp_kernel(xn_ref, x_ref, wu_ref, wd_ref, g_ref, xo_ref, xn_out_ref):
    f = pl.program_id(1)

    @pl.when(f == 0)
    def _():
        xo_ref[...] = x_ref[...]

    h = _dot(xn_ref[...], wu_ref[...].astype(BF16))
    h = jnp.square(jnp.maximum(h, 0.0)).astype(BF16)
    xo_ref[...] += _dot(h, wd_ref[...].astype(BF16))

    @pl.when(f == pl.num_programs(1) - 1)
    def _():
        xn_out_ref[...] = _rms(xo_ref[...], g_ref[...]).astype(xn_out_ref.dtype)


def mlp(xn, x, w_up, w_down, layer, g, out_dtype):
    M, D = x.shape
    F = w_up.shape[2]
    tm = _tile(M, 1024)
    tf = _tile(F, 512)
    row = lambda i, f: (i, 0)
    once = pl.Buffered(1)
    return pl.pallas_call(
        _mlp_kernel, grid=(M // tm, F // tf),
        in_specs=[pl.BlockSpec((tm, D), row), pl.BlockSpec((tm, D), row, pipeline_mode=once),
                  pl.BlockSpec((None, D, tf), lambda i, f: (layer, 0, f)),
                  pl.BlockSpec((None, tf, D), lambda i, f: (layer, f, 0)),
                  pl.BlockSpec((1, D), lambda i, f: (0, 0))],
        out_specs=[pl.BlockSpec((tm, D), row, pipeline_mode=once),
                   pl.BlockSpec((tm, D), row, pipeline_mode=once)],
        out_shape=[jax.ShapeDtypeStruct((M, D), F32), jax.ShapeDtypeStruct((M, D), out_dtype)],
        compiler_params=_cp(("parallel", "arbitrary"), 56), name="mlp")(xn, x, w_up, w_down, g.reshape(1, D))


def _rglru_gates(xc, wg_ref, ba, bx, lam):
    C = xc.shape[1]
    xcb = xc.astype(BF16)
    r_parts, i_parts = [], []
    for c in range(C // GATE_BLOCK):
        lo, hi = c * GATE_BLOCK, (c + 1) * GATE_BLOCK
        gates = _dot(xcb[:, lo:hi], wg_ref[c])
        r_parts.append(gates[:, :GATE_BLOCK])
        i_parts.append(gates[:, GATE_BLOCK:])
    r = _sigmoid(jnp.concatenate(r_parts, axis=1) + ba)
    ig = _sigmoid(jnp.concatenate(i_parts, axis=1) + bx)
    log_a = (-LRU_C * _softplus(-lam)) * r
    a = jnp.exp(log_a)
    u = jnp.sqrt(-jnp.tanh(log_a) * (a * a + 1.0)) * (ig * xc)
    return a, u


def _rglru_prompt_kernel(xr_ref, gr_ref, cw_ref, cb_ref, wg_ref, ba_ref, bx_ref, lam_ref,
                         y_ref, tail_ref, hlast_ref, xp_ref, a_ref, u_ref, h_ref):
    t = pl.program_id(1)
    tT, C = xr_ref.shape
    nk = cw_ref.shape[0]

    @pl.when(t == 0)
    def _():
        xp_ref[0:SUBLANES, :] = jnp.zeros((SUBLANES, C), F32)
        h_ref[...] = jnp.zeros_like(h_ref)

    x = xr_ref[...]
    xp_ref[SUBLANES:SUBLANES + tT, :] = x
    xc = cb_ref[...] + cw_ref[nk - 1:nk, :] * x
    for k in range(nk - 1):
        off = SUBLANES - (nk - 1) + k
        xc = xc + cw_ref[k:k + 1, :] * xp_ref[off:off + tT, :]
    xp_ref[0:SUBLANES, :] = x[tT - SUBLANES:tT, :]

    a, u = _rglru_gates(xc, wg_ref, ba_ref[...], bx_ref[...], lam_ref[...])
    a_ref[...] = a
    u_ref[...] = u

    rowi = lax.broadcasted_iota(jnp.int32, (SUBLANES, C), 0)

    def body(g, h):
        r0 = pl.multiple_of(g * SUBLANES, SUBLANES)
        a8 = a_ref[pl.ds(r0, SUBLANES), :]
        u8 = u_ref[pl.ds(r0, SUBLANES), :]
        for s in (1, 2, 4):
            a_sh = pltpu.roll(a8, s, 0)
            u_sh = pltpu.roll(u8, s, 0)
            m = rowi >= s
            u8 = jnp.where(m, a8 * u_sh + u8, u8)
            a8 = jnp.where(m, a8 * a_sh, a8)
        h8 = a8 * h + u8
        u_ref[pl.ds(r0, SUBLANES), :] = h8
        return h8[SUBLANES - 1:SUBLANES, :]

    h = lax.fori_loop(0, tT // SUBLANES, body, h_ref[...])
    h_ref[...] = h
    y_ref[...] = (u_ref[...] * _gelu_tanh(gr_ref[...])).astype(y_ref.dtype)

    @pl.when(t == pl.num_programs(1) - 1)
    def _():
        tail_ref[0] = x[tT - SUBLANES:tT, :]
        hlast_ref[0] = h


def rglru_prompt(proj, B, T, cw, cb, wg, ba, bx, lam):
    C = cw.shape[1]
    tT = _tile(T, 256)
    nT = T // tT
    fixed2 = lambda b, t: (0, 0)
    vec = pl.BlockSpec((1, C), fixed2)
    return pl.pallas_call(
        _rglru_prompt_kernel, grid=(B, nT),
        in_specs=[pl.BlockSpec((tT, C), lambda b, t: (b * nT + t, 0)),
                  pl.BlockSpec((tT, C), lambda b, t: (b * nT + t, 1)),
                  pl.BlockSpec(cw.shape, fixed2), vec,
                  pl.BlockSpec(wg.shape, lambda b, t: (0, 0, 0)), vec, vec, vec],
        out_specs=[pl.BlockSpec((tT, C), lambda b, t: (b * nT + t, 0)),
                   pl.BlockSpec((1, SUBLANES, C), lambda b, t: (b, 0, 0)),
                   pl.BlockSpec((1, 1, C), lambda b, t: (b, 0, 0))],
        out_shape=[jax.ShapeDtypeStruct((B * T, C), BF16),
                   jax.ShapeDtypeStruct((B, SUBLANES, C), F32),
                   jax.ShapeDtypeStruct((B, 1, C), F32)],
        scratch_shapes=[pltpu.VMEM((SUBLANES + tT, C), F32), pltpu.VMEM((tT, C), F32),
                        pltpu.VMEM((tT, C), F32), pltpu.VMEM((1, C), F32)],
        compiler_params=_cp(("parallel", "arbitrary"), 40), name="rglru_prompt",
    )(proj, proj, cw, cb.reshape(1, C), wg, ba.reshape(1, C), bx.reshape(1, C), lam.reshape(1, C))


CONF_STRIDE = 4
LANES = 128


def _conf_prompt_kernel(ac_ref, bc_ref, dw_ref, db_ref, lg_ref, lb_ref,
                        y_ref, tail_ref, gp_ref, yc_ref):
    t = pl.program_id(1)
    tT, C = ac_ref.shape
    nk = dw_ref.shape[0]
    pad = gp_ref.shape[1] - tT
    nslab = C // LANES
    span = CONF_STRIDE * SUBLANES

    @pl.when(t == 0)
    def _():
        gp_ref[:, 0:pad, :] = jnp.zeros((nslab, pad, LANES), F32)

    glu = ac_ref[...] * _sigmoid(bc_ref[...])
    for c in range(nslab):
        gp_ref[c, pad:pad + tT, :] = glu[:, c * LANES:(c + 1) * LANES]
    for c in range(nslab):
        lo = c * LANES
        for r0 in range(0, tT, span):
            accs = [jnp.broadcast_to(db_ref[:, lo:lo + LANES], (SUBLANES, LANES)) for _ in range(CONF_STRIDE)]
            for k in range(nk):
                w = dw_ref[k:k + 1, lo:lo + LANES]
                off = pad - (nk - 1) + k + r0
                for p in range(CONF_STRIDE):
                    accs[p] = accs[p] + gp_ref[c, pl.ds(off + p, SUBLANES, stride=CONF_STRIDE), :] * w
            for p in range(CONF_STRIDE):
                yc_ref[c, pl.ds(r0 + p, SUBLANES, stride=CONF_STRIDE), :] = accs[p]
    yc = jnp.concatenate([yc_ref[c] for c in range(nslab)], axis=1)
    yn = _layernorm(yc, lg_ref[...], lb_ref[...])
    y_ref[...] = (yn * _sigmoid(yn)).astype(y_ref.dtype)

    @pl.when(t == pl.num_programs(1) - 1)
    def _():
        tail_ref[0] = jnp.concatenate([gp_ref[c, tT:tT + pad, :] for c in range(nslab)], axis=1)

    gp_ref[:, 0:pad, :] = gp_ref[:, tT:tT + pad, :]


def conf_prompt(proj, B, T, dw, db, lg, lb):
    nk, C = dw.shape
    tT = _tile(T, 128)
    nT = T // tT
    pad = -(-(nk - 1) // SUBLANES) * SUBLANES
    assert tT % (CONF_STRIDE * SUBLANES) == 0 and C % LANES == 0
    fixed2 = lambda b, t: (0, 0)
    vec = pl.BlockSpec((1, C), fixed2)
    return pl.pallas_call(
        _conf_prompt_kernel, grid=(B, nT),
        in_specs=[pl.BlockSpec((tT, C), lambda b, t: (b * nT + t, 2)),
                  pl.BlockSpec((tT, C), lambda b, t: (b * nT + t, 3)),
                  pl.BlockSpec((nk, C), fixed2), vec, vec, vec],
        out_specs=[pl.BlockSpec((tT, C), lambda b, t: (b * nT + t, 0)),
                   pl.BlockSpec((1, pad, C), lambda b, t: (b, 0, 0))],
        out_shape=[jax.ShapeDtypeStruct((B * T, C), BF16), jax.ShapeDtypeStruct((B, pad, C), F32)],
        scratch_shapes=[pltpu.VMEM((C // LANES, pad + tT, LANES), F32),
                        pltpu.VMEM((C // LANES, tT, LANES), F32)],
        compiler_params=_cp(("parallel", "arbitrary"), 32), name="conf_prompt",
    )(proj, proj, dw, db.reshape(1, C), lg.reshape(1, C), lb.reshape(1, C))


def _ab_sample_kernel(*refs, has_acc):
    (proj_ref, rconv_ref, h0_ref, cconv_ref,
     cw_ref, cb_ref, wg_ref, ba_ref, bx_ref, lam_ref, dw_ref, db_ref, lg_ref, lb_ref) = refs[:14]
    yr_ref, yc_ref, rconv_o, h_o, cconv_o = refs[14 + int(has_acc):]
    C = h0_ref.shape[1]
    nk = cw_ref.shape[0]
    nd = dw_ref.shape[0]
    x_r = proj_ref[:, 0:C]
    g_r = proj_ref[:, C:2 * C]
    glu = proj_ref[:, 2 * C:3 * C] * _sigmoid(proj_ref[:, 3 * C:4 * C])

    xc = cb_ref[...] + cw_ref[nk - 1:nk, :] * x_r
    for k in range(nk - 1):
        xc = xc + cw_ref[k:k + 1, :] * rconv_ref[k]
    for k in range(nk - 2):
        rconv_o[k] = rconv_ref[k + 1]
    rconv_o[nk - 2] = x_r

    yc = db_ref[...] + dw_ref[nd - 1:nd, :] * glu
    for k in range(nd - 1):
        yc = yc + dw_ref[k:k + 1, :] * cconv_ref[k]
    for k in range(nd - 2):
        cconv_o[k] = cconv_ref[k + 1]
    cconv_o[nd - 2] = glu

    a, u = _rglru_gates(xc, wg_ref, ba_ref[...], bx_ref[...], lam_ref[...])
    h = a * h0_ref[...] + u
    h_o[...] = h
    yr_ref[...] = h * _gelu_tanh(g_r)
    yn = _layernorm(yc, lg_ref[...], lb_ref[...])
    yc_ref[...] = yn * _sigmoid(yn)


def ab_sample(proj, rconv_t, h0, cconv_t, layer, cconv_acc, cw, cb, wg, ba, bx, lam, dw, db, lg, lb):
    _, B, C = h0.shape
    bt = _tile(B, 32)
    row = lambda i: (i, 0)
    fixed = lambda i: (0, 0)
    vec = pl.BlockSpec((1, C), fixed)
    nr, nc = rconv_t.shape[1], cconv_t.shape[1]
    st_c = pl.BlockSpec((None, nc, bt, C), lambda i: (layer, 0, i, 0))
    act = pl.BlockSpec((bt, C), row)
    has_acc = cconv_acc is not None
    in_specs = [pl.BlockSpec((bt, proj.shape[1]), row),
                pl.BlockSpec((None, nr, bt, C), lambda i: (layer, 0, i, 0)),
                pl.BlockSpec((None, bt, C), lambda i: (layer, i, 0)), st_c,
                pl.BlockSpec(cw.shape, fixed), vec, pl.BlockSpec(wg.shape, lambda i: (0, 0, 0)),
                vec, vec, vec, pl.BlockSpec(dw.shape, fixed), vec, vec, vec]
    args = [proj, rconv_t, h0, cconv_t, cw, cb.reshape(1, C), wg, ba.reshape(1, C), bx.reshape(1, C),
            lam.reshape(1, C), dw, db.reshape(1, C), lg.reshape(1, C), lb.reshape(1, C)]
    if has_acc:
        in_specs.append(pl.BlockSpec(memory_space=pl.ANY))
        args.append(cconv_acc)
    return pl.pallas_call(
        functools.partial(_ab_sample_kernel, has_acc=has_acc), grid=(B // bt,),
        in_specs=in_specs,
        out_specs=[act, act, pl.BlockSpec((nr, bt, C), lambda i: (0, i, 0)), act, st_c],
        out_shape=[jax.ShapeDtypeStruct((B, C), F32), jax.ShapeDtypeStruct((B, C), F32),
                   jax.ShapeDtypeStruct((nr, B, C), F32), jax.ShapeDtypeStruct((B, C), F32),
                   jax.ShapeDtypeStruct(cconv_t.shape, F32)],
        input_output_aliases={len(args) - 1: 4} if has_acc else {},
        compiler_params=_cp(("parallel",), 40), name="ab_sample",
    )(*args)


def _mlstm_prompt_kernel(q_ref, k_ref, v_ref, o_ref, g_ref, bif_ref, hng_ref,
                         y_ref, C_ref, n_ref, m_ref, *, heads):
    c = pl.program_id(1)
    L = q_ref.shape[0]
    H = heads
    DK = q_ref.shape[1] // H
    DV = v_ref.shape[1] // H
    scale = DK ** -0.5

    @pl.when(c == 0)
    def _():
        C_ref[...] = jnp.zeros_like(C_ref)
        n_ref[...] = jnp.zeros_like(n_ref)
        m_ref[...] = jnp.zeros_like(m_ref)

    ti = lax.broadcasted_iota(jnp.int32, (L, L), 0)
    si = lax.broadcasted_iota(jnp.int32, (L, L), 1)
    causal = si <= ti
    tri_lo = causal.astype(BF16)
    tri_up = (ti <= si).astype(BF16)

    gates = g_ref[...] + bif_ref[...]
    lf = _log_sigmoid(gates)
    gates_t = gates.T
    lf_t = lf.T
    hi, mid, lo = _split3(lf)
    b_cols = (_dot(tri_lo, lo) + _dot(tri_lo, mid)) + _dot(tri_lo, hi)
    hi, mid, lo = _split3(lf_t)
    b_rows = (_dot(lo, tri_up) + _dot(mid, tri_up)) + _dot(hi, tri_up)

    m_all = m_ref[0]
    n_all = n_ref[0]
    hs = range(H)
    dk = lambda h: slice(h * DK, (h + 1) * DK)
    dv = lambda h: slice(h * DV, (h + 1) * DV)
    ig_col = [gates[:, h:h + 1] for h in hs]
    ig_row = [gates_t[h:h + 1, :] for h in hs]
    b_col = [b_cols[:, H + h:H + h + 1] for h in hs]
    b_row = [b_rows[H + h:H + h + 1, :] for h in hs]
    m_prev = [m_all[h:h + 1, 0:1] for h in hs]
    n_row = [n_all[h:h + 1, :] for h in hs]

    qs = [q_ref[:, dk(h)] * scale for h in hs]
    qb = [qs[h].astype(BF16) for h in hs]
    kb = [k_ref[:, dk(h)].astype(BF16) for h in hs]
    vb = [v_ref[:, dv(h)].astype(BF16) for h in hs]
    dmat = [jnp.where(causal, b_col[h] - b_row[h] + ig_row[h], -jnp.inf) for h in hs]
    inter = [b_col[h] + m_prev[h] for h in hs]
    m_t = [jnp.maximum(inter[h], jnp.max(dmat[h], axis=-1, keepdims=True)) for h in hs]
    qk = [lax.dot_general(qb[h], kb[h], (((1,), (1,)), ((), ())), preferred_element_type=F32) for h in hs]
    s = [qk[h] * jnp.exp(dmat[h] - m_t[h]) for h in hs]
    scale_in = [jnp.exp(inter[h] - m_t[h]) for h in hs]
    qC = [_dot(qb[h], C_ref[0, h].astype(BF16)) for h in hs]
    num = [_dot(s[h].astype(BF16), vb[h]) + scale_in[h] * qC[h] for h in hs]
    den = [jnp.sum(s[h], axis=-1, keepdims=True)
           + scale_in[h] * jnp.sum(qs[h] * n_row[h], axis=-1, keepdims=True) for h in hs]
    hh = [num[h] / jnp.maximum(jnp.abs(den[h]), jnp.exp(-m_t[h])) for h in hs]
    b_last = [b_col[h][L - 1:L, :] for h in hs]
    g_row = [b_last[h] - b_row[h] + ig_row[h] for h in hs]
    g_col = [b_last[h] - b_col[h] + ig_col[h] for h in hs]
    m_new = [jnp.maximum(b_last[h] + m_prev[h], jnp.max(g_row[h], axis=-1, keepdims=True)) for h in hs]
    decay = [jnp.exp(b_last[h] + m_prev[h] - m_new[h]) for h in hs]
    kw = [k_ref[:, dk(h)] * jnp.exp(g_col[h] - m_new[h]) for h in hs]
    for h in hs:
        C_ref[0, h] = decay[h] * C_ref[0, h] + lax.dot_general(
            kw[h].astype(BF16), vb[h], (((0,), (0,)), ((), ())), preferred_element_type=F32)
    n_ref[0] = jnp.concatenate([decay[h] * n_row[h] + jnp.sum(kw[h], axis=0, keepdims=True) for h in hs], axis=0)
    m_ref[0] = jnp.concatenate([jnp.broadcast_to(m_new[h], (1, m_ref.shape[2])) for h in hs], axis=0)
    for h in hs:
        hn = hh[h] * lax.rsqrt(jnp.mean(hh[h] * hh[h], axis=-1, keepdims=True) + EPS)
        y = (hn * hng_ref[:, dv(h)]) * _sigmoid(o_ref[:, dv(h)])
        y_ref[:, dv(h)] = y.astype(y_ref.dtype)


def mlstm_prompt(proj, gates, bif_row, hng, B, T, heads):
    QK = proj.shape[1] // 6
    V = 2 * QK
    H = heads
    DK, DV = QK // H, V // H
    L = ML_CHUNK if T % ML_CHUNK == 0 else T
    nc = T // L
    rowmap = lambda j: (lambda b, c: (b * nc + c, j))
    fixed2 = lambda b, c: (0, 0)
    return pl.pallas_call(
        functools.partial(_mlstm_prompt_kernel, heads=H), grid=(B, nc),
        in_specs=[pl.BlockSpec((L, QK), rowmap(0)), pl.BlockSpec((L, QK), rowmap(1)),
                  pl.BlockSpec((L, V), rowmap(1)), pl.BlockSpec((L, V), rowmap(2)),
                  pl.BlockSpec((L, LANES), rowmap(0)), pl.BlockSpec((1, LANES), fixed2),
                  pl.BlockSpec((1, V), fixed2)],
        out_specs=[pl.BlockSpec((L, V), rowmap(0)),
                   pl.BlockSpec((1, H, DK, DV), lambda b, c: (b, 0, 0, 0)),
                   pl.BlockSpec((1, H, DK), lambda b, c: (b, 0, 0)),
                   pl.BlockSpec((1, H, LANES), lambda b, c: (b, 0, 0))],
        out_shape=[jax.ShapeDtypeStruct((B * T, V), BF16),
                   jax.ShapeDtypeStruct((B, H, DK, DV), F32),
                   jax.ShapeDtypeStruct((B, H, DK), F32),
                   jax.ShapeDtypeStruct((B, H, LANES), F32)],
        compiler_params=_cp(("parallel", "arbitrary"), 40), name="mlstm_prompt",
    )(proj, proj, proj, proj, gates, bif_row, hng.reshape(1, V))


def _mlstm_gate_terms(ig, fg, m):
    lf = _log_sigmoid(fg)
    inter = lf + m
    m_t = jnp.maximum(inter, ig)
    return m_t, jnp.exp(ig - m_t), jnp.exp(inter - m_t)


def _mlstm_sample_kernel(*refs, has_acc):
    (qt_ref, kt_ref, qr_ref, kr_ref, v_ref, o_ref, gr_ref, gc_ref, bifr_ref, bifc_ref,
     hng_ref, C_ref, n_ref) = refs[:13]
    y_ref, Co_ref, no_ref, mo_ref = refs[13 + int(has_acc):]
    bb, H, DK = qr_ref.shape
    scale = DK ** -0.5
    bifr = bifr_ref[...]
    bifc = bifc_ref[...]
    for j in range(bb):
        gr = gr_ref[j]
        gc = gc_ref[j]
        mt_r, es_r, sc_r = _mlstm_gate_terms(gr[:, 0:H] + bifr[:, 0:H], gr[:, H:2 * H] + bifr[:, H:2 * H],
                                             gr[:, 2 * H:3 * H])
        mt_c, es_c, sc_c = _mlstm_gate_terms(gc[0:H, :] + bifc[0:H, :], gc[H:2 * H, :] + bifc[H:2 * H, :],
                                             gc[2 * H:3 * H, :])
        qr = qr_ref[j] * scale
        kr = kr_ref[j]
        n = n_ref[j]
        v = v_ref[j]
        qk = jnp.sum(qr * kr, axis=1, keepdims=True)
        qn = jnp.sum(qr * n, axis=1, keepdims=True)
        qt = qt_ref[j] * scale
        kts = kt_ref[j] * es_r
        rows = []
        for h in range(H):
            Ch = C_ref[j, h]
            rows.append(jnp.sum(Ch * qt[:, h:h + 1], axis=0, keepdims=True))
            Co_ref[j, h] = sc_r[:, h:h + 1] * Ch + kts[:, h:h + 1] * v[h:h + 1, :]
        qC = jnp.concatenate(rows, axis=0)
        s = qk * es_c
        num = s * v + sc_c * qC
        den = s + sc_c * qn
        hh = num / jnp.maximum(jnp.abs(den), jnp.exp(-mt_c))
        hn = hh * lax.rsqrt(jnp.mean(hh * hh, axis=-1, keepdims=True) + EPS)
        y_ref[j] = (hn * hng_ref[...]) * _sigmoid(o_ref[j])
        no_ref[j] = sc_c * n + es_c * kr
        mo_ref[j] = mt_r


def mlstm_sample(q, k, v, o, gates, m, bif, hng, C, n, layer, C_acc):
    B, H, DK = q.shape
    DV = v.shape[2]
    bb = _tile(B, 4)
    gm = jnp.concatenate([gates, m], axis=1)
    qt = jnp.swapaxes(q, 1, 2)
    kt = jnp.swapaxes(k, 1, 2)
    b3 = lambda i: (i, 0, 0)
    fixed = lambda i: (0, 0)
    blk = lambda a: pl.BlockSpec((bb,) + a.shape[1:], b3)
    C_spec = pl.BlockSpec((None, bb, H, DK, DV), lambda i: (layer, i, 0, 0, 0))
    has_acc = C_acc is not None
    in_specs = [blk(qt), blk(kt), blk(q), blk(k), blk(v), blk(o),
                pl.BlockSpec((bb, 1, 3 * H), b3), pl.BlockSpec((bb, 3 * H, 1), b3),
                pl.BlockSpec((1, 2 * H), fixed), pl.BlockSpec((2 * H, 1), fixed),
                pl.BlockSpec((H, DV), fixed), C_spec,
                pl.BlockSpec((None, bb, H, DK), lambda i: (layer, i, 0, 0))]
    args = [qt, kt, q, k, v, o, gm.reshape(B, 1, 3 * H), gm.reshape(B, 3 * H, 1),
            bif.reshape(1, 2 * H), bif.reshape(2 * H, 1), hng.reshape(H, DV), C, n]
    if has_acc:
        in_specs.append(pl.BlockSpec(memory_space=pl.ANY))
        args.append(C_acc)
    return pl.pallas_call(
        functools.partial(_mlstm_sample_kernel, has_acc=has_acc), grid=(B // bb,),
        in_specs=in_specs,
        out_specs=[blk(v), C_spec, pl.BlockSpec((bb, H, DK), b3), pl.BlockSpec((bb, 1, H), b3)],
        out_shape=[jax.ShapeDtypeStruct((B, H, DV), F32), jax.ShapeDtypeStruct(C.shape, F32),
                   jax.ShapeDtypeStruct((B, H, DK), F32), jax.ShapeDtypeStruct((B, 1, H), F32)],
        input_output_aliases={len(args) - 1: 1} if has_acc else {},
        compiler_params=_cp(("parallel",), 40), name="mlstm_sample",
    )(*args)


def _blockdiag(w, per):
    H, d, _ = w.shape
    w4 = w.reshape(H // per, per, d, d)
    eye = jnp.eye(per, dtype=w.dtype)
    return (w4[:, :, :, None, :] * eye[None, :, None, :, None]).reshape(H // per, per * d, per * d)


def _prep_weights(P):
    W = {}
    W['ab_w_out'] = P['ab_w_out'].astype(BF16)
    W['ml_w_out'] = P['ml_w_out'].astype(BF16)
    heads = P['ml_b_if'].shape[1] // 2
    n_main = P['ml_w_in'].shape[2] - 2 * heads
    W['ml_w_in'] = P['ml_w_in'].astype(BF16)
    W['ml_w_gate'] = jnp.pad(P['ml_w_in'][:, :, n_main:], ((0, 0), (0, 0), (0, LANES - 2 * heads)))
    W['ml_bif_row'] = jnp.pad(P['ml_b_if'], ((0, 0), (0, LANES - 2 * heads)))[:, None, :]
    d = P['rglru_wa'].shape[2]
    per = GATE_BLOCK // d
    wa = jax.vmap(lambda w: _blockdiag(w, per))(P['rglru_wa'])
    wx = jax.vmap(lambda w: _blockdiag(w, per))(P['rglru_wx'])
    W['rglru_wg'] = jnp.concatenate([wa, wx], axis=-1).astype(BF16)
    return W


def _run_trunk(x2d, P, W, B, T, states):
    depth = P['mlp_norm'].shape[0]
    heads = P['ml_b_if'].shape[1] // 2
    n_main = P['ml_w_in'].shape[2] - 2 * heads
    o_rconv, o_rh, o_cconv, o_C, o_n, o_m = [], [], [], [], [], []
    cconv_acc = None
    C_acc = None
    x = x2d
    xn = rmsnorm(x, P['ab_norm'][0], BF16)
    for l in range(depth):
        i = l // 2
        if l % 2 == 0:
            proj = matmul(xn, P['ab_w_in'], i, P['ab_w_in'].shape[2], 1024)
            rg = (P['rglru_conv_w'][i], P['rglru_conv_b'][i], W['rglru_wg'][i], P['rglru_ba'][i],
                  P['rglru_bx'][i], P['rglru_lambda'][i])
            cf = (P['conf_dw_w'][i], P['conf_dw_b'][i], P['conf_ln_g'][i], P['conf_ln_b'][i])
            if states is None:
                y_r, tail_r, h_last = rglru_prompt(proj, B, T, *rg)
                y_c, tail_c = conf_prompt(proj, B, T, *cf)
                nr = P['rglru_conv_w'].shape[1] - 1
                ncf = P['conf_dw_w'].shape[1] - 1
                o_rconv.append(tail_r[:, tail_r.shape[1] - nr:])
                o_rh.append(h_last[:, 0])
                o_cconv.append(tail_c[:, tail_c.shape[1] - ncf:])
            else:
                y_r, y_c, s_rconv, s_h, cconv_acc = ab_sample(proj, states[0], states[1], states[2], i,
                                                              cconv_acc, *rg, *cf)
                o_rconv.append(s_rconv); o_rh.append(s_h)
            x, xn = out_proj([y_r, y_c], W['ab_w_out'], i, x, P['mlp_norm'][l])
        else:
            proj = matmul(xn, W['ml_w_in'], i, n_main, 1024)
            gates = matmul(xn, W['ml_w_gate'], i, LANES, LANES)
            hng = P['ml_hnorm_g'][i]
            if states is None:
                y, C, n, m = mlstm_prompt(proj, gates, W['ml_bif_row'][i], hng, B, T, heads)
                o_C.append(C)
                m = m[:, :, 0]
            else:
                QK = n_main // 6
                V = 2 * QK
                q = proj[:, :QK].reshape(B, heads, QK // heads)
                k = proj[:, QK:2 * QK].reshape(B, heads, QK // heads)
                v = proj[:, 2 * QK:2 * QK + V].reshape(B, heads, V // heads)
                o = proj[:, 2 * QK + V:].reshape(B, heads, V // heads)
                y, C_acc, n, m = mlstm_sample(q, k, v, o, gates[:, :2 * heads], states[5][i], P['ml_b_if'][i],
                                              hng, states[3], states[4], i, C_acc)
                y = y.reshape(B, V)
                m = m[:, 0]
            o_n.append(n); o_m.append(m)
            x, xn = out_proj([y], W['ml_w_out'], i, x, P['mlp_norm'][l])
        last = l == depth - 1
        if last:
            g_next = P['final_norm']
        elif (l + 1) % 2 == 0:
            g_next = P['ab_norm'][(l + 1) // 2]
        else:
            g_next = P['ml_norm'][(l + 1) // 2]
        x, xn = mlp(xn, x, P['mlp_w_up'], P['mlp_w_down'], l, g_next, F32 if last else BF16)
    if states is None:
        rconv_out, cconv_out = jnp.stack(o_rconv), jnp.stack(o_cconv)
    else:
        rconv_out, cconv_out = jnp.swapaxes(jnp.stack(o_rconv), 1, 2), jnp.swapaxes(cconv_acc, 1, 2)
    C_out = jnp.stack(o_C) if states is None else C_acc
    return (xn, rconv_out, jnp.stack(o_rh), cconv_out, C_out, jnp.stack(o_n), jnp.stack(o_m))


def kernel(x_prompt, x_sample, state_rglru_conv, state_rglru_h, state_conf_conv, state_mlstm_C, state_mlstm_n, state_mlstm_m, ab_norm, ab_w_in, rglru_conv_w, rglru_conv_b, rglru_wa, rglru_ba, rglru_wx, rglru_bx, rglru_lambda, conf_dw_w, conf_dw_b, conf_ln_g, conf_ln_b, ab_w_out, ml_norm, ml_w_in, ml_b_if, ml_hnorm_g, ml_w_out, mlp_norm, mlp_w_up, mlp_w_down, final_norm):
    P = {'ab_norm': ab_norm, 'ab_w_in': ab_w_in, 'rglru_conv_w': rglru_conv_w, 'rglru_conv_b': rglru_conv_b,
         'rglru_wa': rglru_wa, 'rglru_ba': rglru_ba, 'rglru_wx': rglru_wx, 'rglru_bx': rglru_bx,
         'rglru_lambda': rglru_lambda, 'conf_dw_w': conf_dw_w, 'conf_dw_b': conf_dw_b,
         'conf_ln_g': conf_ln_g, 'conf_ln_b': conf_ln_b, 'ab_w_out': ab_w_out, 'ml_norm': ml_norm,
         'ml_w_in': ml_w_in, 'ml_b_if': ml_b_if, 'ml_hnorm_g': ml_hnorm_g, 'ml_w_out': ml_w_out,
         'mlp_norm': mlp_norm, 'mlp_w_up': mlp_w_up, 'mlp_w_down': mlp_w_down, 'final_norm': final_norm}
    W = _prep_weights(P)
    Bp, Tp, D = x_prompt.shape
    Bs, Ts, _ = x_sample.shape
    assert Ts == 1, "the sample group advances one token per sequence"
    yp, *p_states = _run_trunk(x_prompt.reshape(Bp * Tp, D), P, W, Bp, Tp, None)
    ys, *s_states = _run_trunk(x_sample.reshape(Bs, D), P, W, Bs, 1,
                               (jnp.swapaxes(state_rglru_conv, 1, 2), state_rglru_h,
                                jnp.swapaxes(state_conf_conv, 1, 2),
                                state_mlstm_C, state_mlstm_n, state_mlstm_m))
    return (yp.reshape(Bp, Tp, D), ys.reshape(Bs, Ts, D), *p_states, *s_states)
```

```python
import functools
import math

import jax
import jax.numpy as jnp
from jax import lax
from jax.experimental import pallas as pl
from jax.experimental.pallas import tpu as pltpu

F32 = jnp.float32
BF16 = jnp.bfloat16
EPS = 1e-6
LRU_C = 8.0
ML_CHUNK = 128
GATE_BLOCK = 256
SUBLANES = 8
MIB = 1024 * 1024


def _cp(sem, vmem_mib):
    return pltpu.CompilerParams(dimension_semantics=sem, vmem_limit_bytes=vmem_mib * MIB)


def _tile(n, pref):
    return pref if n % pref == 0 else n


BF16_ROWS = 16
MM_ROWS_CAP = 1100
OUT_ROWS_CAP = 512


def _row_tile(m, cap):
    best = None
    for t in range(BF16_ROWS, min(m, cap) + 1, BF16_ROWS):
        if m % t == 0:
            best = t
    assert best is not None, (m, cap)
    return best


def _rms(x, g):
    return (x * lax.rsqrt(jnp.mean(x * x, axis=-1, keepdims=True) + EPS)) * g


def _sigmoid(x):
    return jax.nn.sigmoid(x)


def _softplus(z):
    return jnp.maximum(z, 0.0) + jnp.log1p(jnp.exp(-jnp.abs(z)))


def _log_sigmoid(x):
    return -_softplus(-x)


def _gelu_tanh(x):
    c = math.sqrt(2.0 / math.pi)
    return x * (0.5 * (1.0 + jnp.tanh(c * (x + 0.044715 * (x * x * x)))))


def _layernorm(x, g, b):
    mu = jnp.mean(x, axis=-1, keepdims=True)
    d = x - mu
    var = jnp.mean(d * d, axis=-1, keepdims=True)
    return (d * lax.rsqrt(var + EPS)) * g + b


def _dot(a, b):
    return jnp.dot(a, b, preferred_element_type=F32)


def _split3(x):
    hi = x.astype(BF16)
    r1 = x - hi.astype(F32)
    mid = r1.astype(BF16)
    lo = (r1 - mid.astype(F32)).astype(BF16)
    return hi, mid, lo


def _entry_kernel(*refs, has_acc):
    x_ref, g_ref = refs[:2]
    xo_ref, xn_ref = refs[2 + 2 * int(has_acc):]
    x = x_ref[...]
    xo_ref[...] = x
    xn_ref[...] = _rms(x, g_ref[...]).astype(xn_ref.dtype)


def entry_norm(x, g, m_all, row0, acc):
    M, D = x.shape
    tm = _tile(M, 512)
    assert row0 % tm == 0
    off = row0 // tm
    has_acc = acc is not None
    in_specs = [pl.BlockSpec((tm, D), lambda i: (i, 0)), pl.BlockSpec((1, D), lambda i: (0, 0))]
    args = [x, g.reshape(1, D)]
    if has_acc:
        in_specs += [pl.BlockSpec(memory_space=pl.ANY)] * 2
        args += list(acc)
    out_spec = pl.BlockSpec((tm, D), lambda i: (i + off, 0))
    return pl.pallas_call(
        functools.partial(_entry_kernel, has_acc=has_acc), grid=(M // tm,),
        in_specs=in_specs, out_specs=[out_spec, out_spec],
        out_shape=[jax.ShapeDtypeStruct((m_all, D), F32), jax.ShapeDtypeStruct((m_all, D), BF16)],
        input_output_aliases={2: 0, 3: 1} if has_acc else {},
        compiler_params=_cp(("parallel",), 32), name="entry_norm")(*args)


def _matmul_kernel(x_ref, w_ref, o_ref, wb_ref):
    @pl.when(pl.program_id(1) == 0)
    def _():
        wb_ref[...] = w_ref[...].astype(BF16)

    o_ref[...] = _dot(x_ref[...], wb_ref[...]).astype(o_ref.dtype)


def matmul(x, w, layer, n_cols, tn):
    M, K = x.shape
    tm = _row_tile(M, MM_ROWS_CAP)
    tn = _tile(n_cols, tn)
    return pl.pallas_call(
        _matmul_kernel, grid=(n_cols // tn, M // tm),
        in_specs=[pl.BlockSpec((tm, K), lambda j, i: (i, 0)),
                  pl.BlockSpec((None, K, tn), lambda j, i: (layer, 0, j))],
        out_specs=pl.BlockSpec((tm, tn), lambda j, i: (i, j)),
        out_shape=jax.ShapeDtypeStruct((M, n_cols), F32),
        scratch_shapes=[pltpu.VMEM((K, tn), BF16)],
        compiler_params=_cp(("parallel", "arbitrary"), 48), name="in_proj")(x, w)


def _outproj_kernel(*refs, n_act):
    acts = refs[:n_act]
    w_ref, x_ref, g_ref, xo_ref, xn_ref = refs[n_act:]
    y = None
    k0 = 0
    for a_ref in acts:
        kw = a_ref.shape[1]
        part = _dot(a_ref[...].astype(BF16), w_ref[k0:k0 + kw, :])
        y = part if y is None else y + part
        k0 += kw
    x_new = x_ref[...] + y
    xo_ref[...] = x_new
    xn_ref[...] = _rms(x_new, g_ref[...]).astype(xn_ref.dtype)


def out_proj(acts, w, layer, x, g):
    M, D = x.shape
    K = w.shape[1]
    tm = _row_tile(M, OUT_ROWS_CAP)
    row = lambda i: (i, 0)
    fixed = lambda i: (0, 0)
    in_specs = [pl.BlockSpec((tm, a.shape[1]), row) for a in acts]
    in_specs += [pl.BlockSpec((None, K, D), lambda i: (layer, 0, 0), pipeline_mode=pl.Buffered(1)),
                 pl.BlockSpec((tm, D), row), pl.BlockSpec((1, D), fixed)]
    return pl.pallas_call(
        functools.partial(_outproj_kernel, n_act=len(acts)), grid=(M // tm,),
        in_specs=in_specs,
        out_specs=[pl.BlockSpec((tm, D), row), pl.BlockSpec((tm, D), row)],
        out_shape=[jax.ShapeDtypeStruct((M, D), F32), jax.ShapeDtypeStruct((M, D), BF16)],
        compiler_params=_cp(("parallel",), 48), name="out_proj")(*acts, w, x, g.reshape(1, D))


def _mlp_kernel(xn_ref, x_ref, wu_ref, wd_ref, g_ref, xo_ref, xn_out_ref):
    f = pl.program_id(1)

    @pl.when(f == 0)
    def _():
        xo_ref[...] = x_ref[...]

    h = _dot(xn_ref[...], wu_ref[...].astype(BF16))
    h = jnp.square(jnp.maximum(h, 0.0)).astype(BF16)
    xo_ref[...] += _dot(h, wd_ref[...].astype(BF16))

    @pl.when(f == pl.num_programs(1) - 1)
    def _():
        xn_out_ref[...] = _rms(xo_ref[...], g_ref[...]).astype(xn_out_ref.dtype)


def mlp(xn, x, w_up, w_down, layer, g, out_dtype, row0=0, nrows=None):
    D = x.shape[1]
    M = x.shape[0] if nrows is None else nrows
    F = w_up.shape[2]
    tm = _row_tile(M, MM_ROWS_CAP)
    assert row0 % tm == 0
    off = row0 // tm
    tf = _tile(F, 512)
    row_in = lambda i, f: (i + off, 0)
    row = lambda i, f: (i, 0)
    once = pl.Buffered(1)
    return pl.pallas_call(
        _mlp_kernel, grid=(M // tm, F // tf),
        in_specs=[pl.BlockSpec((tm, D), row_in), pl.BlockSpec((tm, D), row_in, pipeline_mode=once),
                  pl.BlockSpec((None, D, tf), lambda i, f: (layer, 0, f)),
                  pl.BlockSpec((None, tf, D), lambda i, f: (layer, f, 0)),
                  pl.BlockSpec((1, D), lambda i, f: (0, 0))],
        out_specs=[pl.BlockSpec((tm, D), row, pipeline_mode=once),
                   pl.BlockSpec((tm, D), row, pipeline_mode=once)],
        out_shape=[jax.ShapeDtypeStruct((M, D), F32), jax.ShapeDtypeStruct((M, D), out_dtype)],
        compiler_params=_cp(("parallel", "arbitrary"), 56), name="mlp")(xn, x, w_up, w_down, g.reshape(1, D))


def _rglru_gates(xc, wg_ref, ba, bx, lam):
    C = xc.shape[1]
    xcb = xc.astype(BF16)
    r_parts, i_parts = [], []
    for c in range(C // GATE_BLOCK):
        lo, hi = c * GATE_BLOCK, (c + 1) * GATE_BLOCK
        gates = _dot(xcb[:, lo:hi], wg_ref[c])
        r_parts.append(gates[:, :GATE_BLOCK])
        i_parts.append(gates[:, GATE_BLOCK:])
    r = _sigmoid(jnp.concatenate(r_parts, axis=1) + ba)
    ig = _sigmoid(jnp.concatenate(i_parts, axis=1) + bx)
    log_a = (-LRU_C * _softplus(-lam)) * r
    a = jnp.exp(log_a)
    u = jnp.sqrt(-jnp.tanh(log_a) * (a * a + 1.0)) * (ig * xc)
    return a, u


def _rglru_prompt_kernel(xr_ref, gr_ref, cw_ref, cb_ref, wg_ref, ba_ref, bx_ref, lam_ref,
                         y_ref, tail_ref, hlast_ref, xp_ref, a_ref, u_ref, h_ref):
    t = pl.program_id(1)
    tT, C = xr_ref.shape
    nk = cw_ref.shape[0]

    @pl.when(t == 0)
    def _():
        xp_ref[0:SUBLANES, :] = jnp.zeros((SUBLANES, C), F32)
        h_ref[...] = jnp.zeros_like(h_ref)

    x = xr_ref[...]
    xp_ref[SUBLANES:SUBLANES + tT, :] = x
    xc = cb_ref[...] + cw_ref[nk - 1:nk, :] * x
    for k in range(nk - 1):
        off = SUBLANES - (nk - 1) + k
        xc = xc + cw_ref[k:k + 1, :] * xp_ref[off:off + tT, :]
    xp_ref[0:SUBLANES, :] = x[tT - SUBLANES:tT, :]

    a, u = _rglru_gates(xc, wg_ref, ba_ref[...], bx_ref[...], lam_ref[...])
    a_ref[...] = a
    u_ref[...] = u

    rowi = lax.broadcasted_iota(jnp.int32, (SUBLANES, C), 0)

    def body(g, h):
        r0 = pl.multiple_of(g * SUBLANES, SUBLANES)
        a8 = a_ref[pl.ds(r0, SUBLANES), :]
        u8 = u_ref[pl.ds(r0, SUBLANES), :]
        for s in (1, 2, 4):
            a_sh = pltpu.roll(a8, s, 0)
            u_sh = pltpu.roll(u8, s, 0)
            m = rowi >= s
            u8 = jnp.where(m, a8 * u_sh + u8, u8)
            a8 = jnp.where(m, a8 * a_sh, a8)
        h8 = a8 * h + u8
        u_ref[pl.ds(r0, SUBLANES), :] = h8
        return h8[SUBLANES - 1:SUBLANES, :]

    h = lax.fori_loop(0, tT // SUBLANES, body, h_ref[...])
    h_ref[...] = h
    y_ref[...] = (u_ref[...] * _gelu_tanh(gr_ref[...])).astype(y_ref.dtype)

    @pl.when(t == pl.num_programs(1) - 1)
    def _():
        tail_ref[0] = x[tT - SUBLANES:tT, :]
        hlast_ref[0] = h


def rglru_prompt(proj, B, T, cw, cb, wg, ba, bx, lam):
    C = cw.shape[1]
    tT = _tile(T, 256)
    nT = T // tT
    fixed2 = lambda b, t: (0, 0)
    vec = pl.BlockSpec((1, C), fixed2)
    return pl.pallas_call(
        _rglru_prompt_kernel, grid=(B, nT),
        in_specs=[pl.BlockSpec((tT, C), lambda b, t: (b * nT + t, 0)),
                  pl.BlockSpec((tT, C), lambda b, t: (b * nT + t, 1)),
                  pl.BlockSpec(cw.shape, fixed2), vec,
                  pl.BlockSpec(wg.shape, lambda b, t: (0, 0, 0)), vec, vec, vec],
        out_specs=[pl.BlockSpec((tT, C), lambda b, t: (b * nT + t, 0)),
                   pl.BlockSpec((1, SUBLANES, C), lambda b, t: (b, 0, 0)),
                   pl.BlockSpec((1, 1, C), lambda b, t: (b, 0, 0))],
        out_shape=[jax.ShapeDtypeStruct((proj.shape[0], C), BF16),
                   jax.ShapeDtypeStruct((B, SUBLANES, C), F32),
                   jax.ShapeDtypeStruct((B, 1, C), F32)],
        scratch_shapes=[pltpu.VMEM((SUBLANES + tT, C), F32), pltpu.VMEM((tT, C), F32),
                        pltpu.VMEM((tT, C), F32), pltpu.VMEM((1, C), F32)],
        compiler_params=_cp(("parallel", "arbitrary"), 40), name="rglru_prompt",
    )(proj, proj, cw, cb.reshape(1, C), wg, ba.reshape(1, C), bx.reshape(1, C), lam.reshape(1, C))


CONF_STRIDE = 4
LANES = 128


def _conf_prompt_kernel(ac_ref, bc_ref, dw_ref, db_ref, lg_ref, lb_ref,
                        y_ref, tail_ref, gp_ref, yc_ref):
    t = pl.program_id(1)
    tT, C = ac_ref.shape
    nk = dw_ref.shape[0]
    pad = gp_ref.shape[1] - tT
    nslab = C // LANES
    span = CONF_STRIDE * SUBLANES

    @pl.when(t == 0)
    def _():
        gp_ref[:, 0:pad, :] = jnp.zeros((nslab, pad, LANES), F32)

    glu = ac_ref[...] * _sigmoid(bc_ref[...])
    for c in range(nslab):
        gp_ref[c, pad:pad + tT, :] = glu[:, c * LANES:(c + 1) * LANES]
    for c in range(nslab):
        lo = c * LANES
        for r0 in range(0, tT, span):
            accs = [jnp.broadcast_to(db_ref[:, lo:lo + LANES], (SUBLANES, LANES)) for _ in range(CONF_STRIDE)]
            for k in range(nk):
                w = dw_ref[k:k + 1, lo:lo + LANES]
                off = pad - (nk - 1) + k + r0
                for p in range(CONF_STRIDE):
                    accs[p] = accs[p] + gp_ref[c, pl.ds(off + p, SUBLANES, stride=CONF_STRIDE), :] * w
            for p in range(CONF_STRIDE):
                yc_ref[c, pl.ds(r0 + p, SUBLANES, stride=CONF_STRIDE), :] = accs[p]
    yc = jnp.concatenate([yc_ref[c] for c in range(nslab)], axis=1)
    yn = _layernorm(yc, lg_ref[...], lb_ref[...])
    y_ref[...] = (yn * _sigmoid(yn)).astype(y_ref.dtype)

    @pl.when(t == pl.num_programs(1) - 1)
    def _():
        tail_ref[0] = jnp.concatenate([gp_ref[c, tT:tT + pad, :] for c in range(nslab)], axis=1)

    gp_ref[:, 0:pad, :] = gp_ref[:, tT:tT + pad, :]


def conf_prompt(proj, B, T, dw, db, lg, lb):
    nk, C = dw.shape
    tT = _tile(T, 128)
    nT = T // tT
    pad = -(-(nk - 1) // SUBLANES) * SUBLANES
    assert tT % (CONF_STRIDE * SUBLANES) == 0 and C % LANES == 0
    fixed2 = lambda b, t: (0, 0)
    vec = pl.BlockSpec((1, C), fixed2)
    return pl.pallas_call(
        _conf_prompt_kernel, grid=(B, nT),
        in_specs=[pl.BlockSpec((tT, C), lambda b, t: (b * nT + t, 2)),
                  pl.BlockSpec((tT, C), lambda b, t: (b * nT + t, 3)),
                  pl.BlockSpec((nk, C), fixed2), vec, vec, vec],
        out_specs=[pl.BlockSpec((tT, C), lambda b, t: (b * nT + t, 0)),
                   pl.BlockSpec((1, pad, C), lambda b, t: (b, 0, 0))],
        out_shape=[jax.ShapeDtypeStruct((proj.shape[0], C), BF16), jax.ShapeDtypeStruct((B, pad, C), F32)],
        scratch_shapes=[pltpu.VMEM((C // LANES, pad + tT, LANES), F32),
                        pltpu.VMEM((C // LANES, tT, LANES), F32)],
        compiler_params=_cp(("parallel", "arbitrary"), 32), name="conf_prompt",
    )(proj, proj, dw, db.reshape(1, C), lg.reshape(1, C), lb.reshape(1, C))


def _ab_sample_kernel(*refs, has_acc):
    (proj_ref, rconv_ref, h0_ref, cconv_ref,
     cw_ref, cb_ref, wg_ref, ba_ref, bx_ref, lam_ref, dw_ref, db_ref, lg_ref, lb_ref) = refs[:14]
    yr_ref, yc_ref, rconv_o, h_o, cconv_o = refs[14 + int(has_acc):]
    C = h0_ref.shape[1]
    nk = cw_ref.shape[0]
    nd = dw_ref.shape[0]
    x_r = proj_ref[:, 0:C]
    g_r = proj_ref[:, C:2 * C]
    glu = proj_ref[:, 2 * C:3 * C] * _sigmoid(proj_ref[:, 3 * C:4 * C])

    xc = cb_ref[...] + cw_ref[nk - 1:nk, :] * x_r
    for k in range(nk - 1):
        xc = xc + cw_ref[k:k + 1, :] * rconv_ref[k]
    for k in range(nk - 2):
        rconv_o[k] = rconv_ref[k + 1]
    rconv_o[nk - 2] = x_r

    yc = db_ref[...] + dw_ref[nd - 1:nd, :] * glu
    for k in range(nd - 1):
        yc = yc + dw_ref[k:k + 1, :] * cconv_ref[k]
    for k in range(nd - 2):
        cconv_o[k] = cconv_ref[k + 1]
    cconv_o[nd - 2] = glu

    a, u = _rglru_gates(xc, wg_ref, ba_ref[...], bx_ref[...], lam_ref[...])
    h = a * h0_ref[...] + u
    h_o[...] = h
    yr_ref[...] = h * _gelu_tanh(g_r)
    yn = _layernorm(yc, lg_ref[...], lb_ref[...])
    yc_ref[...] = yn * _sigmoid(yn)


def ab_sample(proj, row0, rconv_t, h0, cconv_t, layer, cconv_acc, cw, cb, wg, ba, bx, lam, dw, db, lg, lb):
    _, B, C = h0.shape
    bt = _tile(B, 32)
    row = lambda i: (i, 0)
    fixed = lambda i: (0, 0)
    vec = pl.BlockSpec((1, C), fixed)
    nr, nc = rconv_t.shape[1], cconv_t.shape[1]
    st_c = pl.BlockSpec((None, nc, bt, C), lambda i: (layer, 0, i, 0))
    act = pl.BlockSpec((bt, C), row)
    has_acc = cconv_acc is not None
    assert row0 % bt == 0
    off = row0 // bt
    in_specs = [pl.BlockSpec((bt, proj.shape[1]), lambda i: (i + off, 0)),
                pl.BlockSpec((None, nr, bt, C), lambda i: (layer, 0, i, 0)),
                pl.BlockSpec((None, bt, C), lambda i: (layer, i, 0)), st_c,
                pl.BlockSpec(cw.shape, fixed), vec, pl.BlockSpec(wg.shape, lambda i: (0, 0, 0)),
                vec, vec, vec, pl.BlockSpec(dw.shape, fixed), vec, vec, vec]
    args = [proj, rconv_t, h0, cconv_t, cw, cb.reshape(1, C), wg, ba.reshape(1, C), bx.reshape(1, C),
            lam.reshape(1, C), dw, db.reshape(1, C), lg.reshape(1, C), lb.reshape(1, C)]
    if has_acc:
        in_specs.append(pl.BlockSpec(memory_space=pl.ANY))
        args.append(cconv_acc)
    return pl.pallas_call(
        functools.partial(_ab_sample_kernel, has_acc=has_acc), grid=(B // bt,),
        in_specs=in_specs,
        out_specs=[act, act, pl.BlockSpec((nr, bt, C), lambda i: (0, i, 0)), act, st_c],
        out_shape=[jax.ShapeDtypeStruct((B, C), F32), jax.ShapeDtypeStruct((B, C), F32),
                   jax.ShapeDtypeStruct((nr, B, C), F32), jax.ShapeDtypeStruct((B, C), F32),
                   jax.ShapeDtypeStruct(cconv_t.shape, F32)],
        input_output_aliases={len(args) - 1: 4} if has_acc else {},
        compiler_params=_cp(("parallel",), 40), name="ab_sample",
    )(*args)


def _mlstm_prompt_kernel(q_ref, k_ref, v_ref, o_ref, g_ref, bif_ref, hng_ref,
                         y_ref, C_ref, n_ref, m_ref, *, heads):
    c = pl.program_id(1)
    L = q_ref.shape[0]
    H = heads
    DK = q_ref.shape[1] // H
    DV = v_ref.shape[1] // H
    scale = DK ** -0.5

    @pl.when(c == 0)
    def _():
        C_ref[...] = jnp.zeros_like(C_ref)
        n_ref[...] = jnp.zeros_like(n_ref)
        m_ref[...] = jnp.zeros_like(m_ref)

    ti = lax.broadcasted_iota(jnp.int32, (L, L), 0)
    si = lax.broadcasted_iota(jnp.int32, (L, L), 1)
    causal = si <= ti
    tri_lo = causal.astype(BF16)
    tri_up = (ti <= si).astype(BF16)

    gates = g_ref[...] + bif_ref[...]
    lf = _log_sigmoid(gates)
    gates_t = gates.T
    lf_t = lf.T
    hi, mid, lo = _split3(lf)
    b_cols = (_dot(tri_lo, lo) + _dot(tri_lo, mid)) + _dot(tri_lo, hi)
    hi, mid, lo = _split3(lf_t)
    b_rows = (_dot(lo, tri_up) + _dot(mid, tri_up)) + _dot(hi, tri_up)

    m_all = m_ref[0]
    n_all = n_ref[0]
    hs = range(H)
    dk = lambda h: slice(h * DK, (h + 1) * DK)
    dv = lambda h: slice(h * DV, (h + 1) * DV)
    ig_col = [gates[:, h:h + 1] for h in hs]
    ig_row = [gates_t[h:h + 1, :] for h in hs]
    b_col = [b_cols[:, H + h:H + h + 1] for h in hs]
    b_row = [b_rows[H + h:H + h + 1, :] for h in hs]
    m_prev = [m_all[h:h + 1, 0:1] for h in hs]
    n_row = [n_all[h:h + 1, :] for h in hs]

    qs = [q_ref[:, dk(h)] * scale for h in hs]
    qb = [qs[h].astype(BF16) for h in hs]
    kb = [k_ref[:, dk(h)].astype(BF16) for h in hs]
    vb = [v_ref[:, dv(h)].astype(BF16) for h in hs]
    dmat = [jnp.where(causal, b_col[h] - b_row[h] + ig_row[h], -jnp.inf) for h in hs]
    inter = [b_col[h] + m_prev[h] for h in hs]
    m_t = [jnp.maximum(inter[h], jnp.max(dmat[h], axis=-1, keepdims=True)) for h in hs]
    qk = [lax.dot_general(qb[h], kb[h], (((1,), (1,)), ((), ())), preferred_element_type=F32) for h in hs]
    s = [qk[h] * jnp.exp(dmat[h] - m_t[h]) for h in hs]
    scale_in = [jnp.exp(inter[h] - m_t[h]) for h in hs]
    qC = [_dot(qb[h], C_ref[0, h].astype(BF16)) for h in hs]
    num = [_dot(s[h].astype(BF16), vb[h]) + scale_in[h] * qC[h] for h in hs]
    den = [jnp.sum(s[h], axis=-1, keepdims=True)
           + scale_in[h] * jnp.sum(qs[h] * n_row[h], axis=-1, keepdims=True) for h in hs]
    hh = [num[h] / jnp.maximum(jnp.abs(den[h]), jnp.exp(-m_t[h])) for h in hs]
    b_last = [b_col[h][L - 1:L, :] for h in hs]
    g_row = [b_last[h] - b_row[h] + ig_row[h] for h in hs]
    g_col = [b_last[h] - b_col[h] + ig_col[h] for h in hs]
    m_new = [jnp.maximum(b_last[h] + m_prev[h], jnp.max(g_row[h], axis=-1, keepdims=True)) for h in hs]
    decay = [jnp.exp(b_last[h] + m_prev[h] - m_new[h]) for h in hs]
    kw = [k_ref[:, dk(h)] * jnp.exp(g_col[h] - m_new[h]) for h in hs]
    for h in hs:
        C_ref[0, h] = decay[h] * C_ref[0, h] + lax.dot_general(
            kw[h].astype(BF16), vb[h], (((0,), (0,)), ((), ())), preferred_element_type=F32)
    n_ref[0] = jnp.concatenate([decay[h] * n_row[h] + jnp.sum(kw[h], axis=0, keepdims=True) for h in hs], axis=0)
    m_ref[0] = jnp.concatenate([jnp.broadcast_to(m_new[h], (1, m_ref.shape[2])) for h in hs], axis=0)
    for h in hs:
        hn = hh[h] * lax.rsqrt(jnp.mean(hh[h] * hh[h], axis=-1, keepdims=True) + EPS)
        y = (hn * hng_ref[:, dv(h)]) * _sigmoid(o_ref[:, dv(h)])
        y_ref[:, dv(h)] = y.astype(y_ref.dtype)


def mlstm_prompt(proj, gates, bif_row, hng, B, T, heads):
    QK = proj.shape[1] // 6
    V = 2 * QK
    H = heads
    DK, DV = QK // H, V // H
    L = ML_CHUNK if T % ML_CHUNK == 0 else T
    nc = T // L
    rowmap = lambda j: (lambda b, c: (b * nc + c, j))
    fixed2 = lambda b, c: (0, 0)
    return pl.pallas_call(
        functools.partial(_mlstm_prompt_kernel, heads=H), grid=(B, nc),
        in_specs=[pl.BlockSpec((L, QK), rowmap(0)), pl.BlockSpec((L, QK), rowmap(1)),
                  pl.BlockSpec((L, V), rowmap(1)), pl.BlockSpec((L, V), rowmap(2)),
                  pl.BlockSpec((L, LANES), rowmap(0)), pl.BlockSpec((1, LANES), fixed2),
                  pl.BlockSpec((1, V), fixed2)],
        out_specs=[pl.BlockSpec((L, V), rowmap(0)),
                   pl.BlockSpec((1, H, DK, DV), lambda b, c: (b, 0, 0, 0)),
                   pl.BlockSpec((1, H, DK), lambda b, c: (b, 0, 0)),
                   pl.BlockSpec((1, H, LANES), lambda b, c: (b, 0, 0))],
        out_shape=[jax.ShapeDtypeStruct((proj.shape[0], V), BF16),
                   jax.ShapeDtypeStruct((B, H, DK, DV), F32),
                   jax.ShapeDtypeStruct((B, H, DK), F32),
                   jax.ShapeDtypeStruct((B, H, LANES), F32)],
        compiler_params=_cp(("parallel", "arbitrary"), 40), name="mlstm_prompt",
    )(proj, proj, proj, proj, gates, bif_row, hng.reshape(1, V))


def _mlstm_gate_terms(ig, fg, m):
    lf = _log_sigmoid(fg)
    inter = lf + m
    m_t = jnp.maximum(inter, ig)
    return m_t, jnp.exp(ig - m_t), jnp.exp(inter - m_t)


def _mlstm_sample_kernel(*refs, has_acc):
    (qt_ref, kt_ref, qr_ref, kr_ref, v_ref, o_ref, gr_ref, gc_ref, bifr_ref, bifc_ref,
     hng_ref, C_ref, n_ref) = refs[:13]
    y_ref, Co_ref, no_ref, mo_ref = refs[13 + int(has_acc):]
    bb, H, DK = qr_ref.shape
    scale = DK ** -0.5
    bifr = bifr_ref[...]
    bifc = bifc_ref[...]
    for j in range(bb):
        gr = gr_ref[j]
        gc = gc_ref[j]
        mt_r, es_r, sc_r = _mlstm_gate_terms(gr[:, 0:H] + bifr[:, 0:H], gr[:, H:2 * H] + bifr[:, H:2 * H],
                                             gr[:, 2 * H:3 * H])
        mt_c, es_c, sc_c = _mlstm_gate_terms(gc[0:H, :] + bifc[0:H, :], gc[H:2 * H, :] + bifc[H:2 * H, :],
                                             gc[2 * H:3 * H, :])
        qr = qr_ref[j] * scale
        kr = kr_ref[j]
        n = n_ref[j]
        v = v_ref[j]
        qk = jnp.sum(qr * kr, axis=1, keepdims=True)
        qn = jnp.sum(qr * n, axis=1, keepdims=True)
        qt = qt_ref[j] * scale
        kts = kt_ref[j] * es_r
        rows = []
        for h in range(H):
            Ch = C_ref[j, h]
            rows.append(jnp.sum(Ch * qt[:, h:h + 1], axis=0, keepdims=True))
            Co_ref[j, h] = sc_r[:, h:h + 1] * Ch + kts[:, h:h + 1] * v[h:h + 1, :]
        qC = jnp.concatenate(rows, axis=0)
        s = qk * es_c
        num = s * v + sc_c * qC
        den = s + sc_c * qn
        hh = num / jnp.maximum(jnp.abs(den), jnp.exp(-mt_c))
        hn = hh * lax.rsqrt(jnp.mean(hh * hh, axis=-1, keepdims=True) + EPS)
        y_ref[j] = (hn * hng_ref[...]) * _sigmoid(o_ref[j])
        no_ref[j] = sc_c * n + es_c * kr
        mo_ref[j] = mt_r


def mlstm_sample(q, k, v, o, gates, m, bif, hng, C, n, layer, C_acc):
    B, H, DK = q.shape
    DV = v.shape[2]
    bb = _tile(B, 4)
    gm = jnp.concatenate([gates, m], axis=1)
    qt = jnp.swapaxes(q, 1, 2)
    kt = jnp.swapaxes(k, 1, 2)
    b3 = lambda i: (i, 0, 0)
    fixed = lambda i: (0, 0)
    blk = lambda a: pl.BlockSpec((bb,) + a.shape[1:], b3)
    C_spec = pl.BlockSpec((None, bb, H, DK, DV), lambda i: (layer, i, 0, 0, 0))
    has_acc = C_acc is not None
    in_specs = [blk(qt), blk(kt), blk(q), blk(k), blk(v), blk(o),
                pl.BlockSpec((bb, 1, 3 * H), b3), pl.BlockSpec((bb, 3 * H, 1), b3),
                pl.BlockSpec((1, 2 * H), fixed), pl.BlockSpec((2 * H, 1), fixed),
                pl.BlockSpec((H, DV), fixed), C_spec,
                pl.BlockSpec((None, bb, H, DK), lambda i: (layer, i, 0, 0))]
    args = [qt, kt, q, k, v, o, gm.reshape(B, 1, 3 * H), gm.reshape(B, 3 * H, 1),
            bif.reshape(1, 2 * H), bif.reshape(2 * H, 1), hng.reshape(H, DV), C, n]
    if has_acc:
        in_specs.append(pl.BlockSpec(memory_space=pl.ANY))
        args.append(C_acc)
    return pl.pallas_call(
        functools.partial(_mlstm_sample_kernel, has_acc=has_acc), grid=(B // bb,),
        in_specs=in_specs,
        out_specs=[blk(v), C_spec, pl.BlockSpec((bb, H, DK), b3), pl.BlockSpec((bb, 1, H), b3)],
        out_shape=[jax.ShapeDtypeStruct((B, H, DV), F32), jax.ShapeDtypeStruct(C.shape, F32),
                   jax.ShapeDtypeStruct((B, H, DK), F32), jax.ShapeDtypeStruct((B, 1, H), F32)],
        input_output_aliases={len(args) - 1: 1} if has_acc else {},
        compiler_params=_cp(("parallel",), 40), name="mlstm_sample",
    )(*args)


def _blockdiag(w, per):
    H, d, _ = w.shape
    w4 = w.reshape(H // per, per, d, d)
    eye = jnp.eye(per, dtype=w.dtype)
    return (w4[:, :, :, None, :] * eye[None, :, None, :, None]).reshape(H // per, per * d, per * d)


def _prep_weights(P):
    W = {}
    W['ab_w_out'] = P['ab_w_out'].astype(BF16)
    W['ml_w_out'] = P['ml_w_out'].astype(BF16)
    heads = P['ml_b_if'].shape[1] // 2
    n_main = P['ml_w_in'].shape[2] - 2 * heads
    W['ml_w_in'] = P['ml_w_in'].astype(BF16)
    W['ml_w_gate'] = jnp.pad(W['ml_w_in'][:, :, n_main:], ((0, 0), (0, 0), (0, LANES - 2 * heads)))
    W['ml_bif_row'] = jnp.pad(P['ml_b_if'], ((0, 0), (0, LANES - 2 * heads)))[:, None, :]
    d = P['rglru_wa'].shape[2]
    per = GATE_BLOCK // d
    wa = jax.vmap(lambda w: _blockdiag(w, per))(P['rglru_wa'])
    wx = jax.vmap(lambda w: _blockdiag(w, per))(P['rglru_wx'])
    W['rglru_wg'] = jnp.concatenate([wa, wx], axis=-1).astype(BF16)
    return W


def _set_rows(y_all, y_rows, row0):
    return lax.dynamic_update_slice(y_all, y_rows.astype(y_all.dtype), (row0, 0))


def _run_trunk(xp, xs, P, W, Bp, Tp, states):
    depth = P['mlp_norm'].shape[0]
    heads = P['ml_b_if'].shape[1] // 2
    n_main = P['ml_w_in'].shape[2] - 2 * heads
    Mp, Bs = xp.shape[0], xs.shape[0]
    m_all = Mp + Bs
    p_rconv, p_rh, p_cconv, p_C, p_n, p_m = [], [], [], [], [], []
    s_rconv, s_rh, s_n, s_m = [], [], [], []
    cconv_acc = None
    C_acc = None
    x, xn = entry_norm(xs, P['ab_norm'][0], m_all, Mp, entry_norm(xp, P['ab_norm'][0], m_all, 0, None))
    for l in range(depth):
        i = l // 2
        if l % 2 == 0:
            proj = matmul(xn, P['ab_w_in'], i, P['ab_w_in'].shape[2], 1024)
            rg = (P['rglru_conv_w'][i], P['rglru_conv_b'][i], W['rglru_wg'][i], P['rglru_ba'][i],
                  P['rglru_bx'][i], P['rglru_lambda'][i])
            cf = (P['conf_dw_w'][i], P['conf_dw_b'][i], P['conf_ln_g'][i], P['conf_ln_b'][i])
            y_r, tail_r, h_last = rglru_prompt(proj, Bp, Tp, *rg)
            y_c, tail_c = conf_prompt(proj, Bp, Tp, *cf)
            nr = P['rglru_conv_w'].shape[1] - 1
            ncf = P['conf_dw_w'].shape[1] - 1
            p_rconv.append(tail_r[:, tail_r.shape[1] - nr:])
            p_rh.append(h_last[:, 0])
            p_cconv.append(tail_c[:, tail_c.shape[1] - ncf:])
            ys_r, ys_c, rconv_new, h_new, cconv_acc = ab_sample(proj, Mp, states[0], states[1], states[2], i,
                                                                cconv_acc, *rg, *cf)
            s_rconv.append(rconv_new); s_rh.append(h_new)
            acts = [_set_rows(y_r, ys_r, Mp), _set_rows(y_c, ys_c, Mp)]
            x, xn = out_proj(acts, W['ab_w_out'], i, x, P['mlp_norm'][l])
        else:
            proj = matmul(xn, W['ml_w_in'], i, n_main, 1024)
            gates = matmul(xn, W['ml_w_gate'], i, LANES, LANES)
            hng = P['ml_hnorm_g'][i]
            y, C, n, m = mlstm_prompt(proj, gates, W['ml_bif_row'][i], hng, Bp, Tp, heads)
            p_C.append(C); p_n.append(n); p_m.append(m[:, :, 0])
            QK = n_main // 6
            V = 2 * QK
            ps = proj[Mp:]
            q = ps[:, :QK].reshape(Bs, heads, QK // heads)
            k = ps[:, QK:2 * QK].reshape(Bs, heads, QK // heads)
            v = ps[:, 2 * QK:2 * QK + V].reshape(Bs, heads, V // heads)
            o = ps[:, 2 * QK + V:].reshape(Bs, heads, V // heads)
            ys, C_acc, n, m = mlstm_sample(q, k, v, o, gates[Mp:, :2 * heads], states[5][i], P['ml_b_if'][i],
                                           hng, states[3], states[4], i, C_acc)
            s_n.append(n); s_m.append(m[:, 0])
            x, xn = out_proj([_set_rows(y, ys.reshape(Bs, V), Mp)], W['ml_w_out'], i, x, P['mlp_norm'][l])
        if l < depth - 1:
            g_next = P['ab_norm'][(l + 1) // 2] if (l + 1) % 2 == 0 else P['ml_norm'][(l + 1) // 2]
            x, xn = mlp(xn, x, P['mlp_w_up'], P['mlp_w_down'], l, g_next, BF16)
    last = depth - 1
    _, yp = mlp(xn, x, P['mlp_w_up'], P['mlp_w_down'], last, P['final_norm'], F32, 0, Mp)
    _, ys = mlp(xn, x, P['mlp_w_up'], P['mlp_w_down'], last, P['final_norm'], F32, Mp, Bs)
    prompt_out = (jnp.stack(p_rconv), jnp.stack(p_rh), jnp.stack(p_cconv),
                  jnp.stack(p_C), jnp.stack(p_n), jnp.stack(p_m))
    sample_out = (jnp.swapaxes(jnp.stack(s_rconv), 1, 2), jnp.stack(s_rh), jnp.swapaxes(cconv_acc, 1, 2),
                  C_acc, jnp.stack(s_n), jnp.stack(s_m))
    return yp, ys, prompt_out, sample_out


def kernel(x_prompt, x_sample, state_rglru_conv, state_rglru_h, state_conf_conv, state_mlstm_C, state_mlstm_n, state_mlstm_m, ab_norm, ab_w_in, rglru_conv_w, rglru_conv_b, rglru_wa, rglru_ba, rglru_wx, rglru_bx, rglru_lambda, conf_dw_w, conf_dw_b, conf_ln_g, conf_ln_b, ab_w_out, ml_norm, ml_w_in, ml_b_if, ml_hnorm_g, ml_w_out, mlp_norm, mlp_w_up, mlp_w_down, final_norm):
    P = {'ab_norm': ab_norm, 'ab_w_in': ab_w_in, 'rglru_conv_w': rglru_conv_w, 'rglru_conv_b': rglru_conv_b,
         'rglru_wa': rglru_wa, 'rglru_ba': rglru_ba, 'rglru_wx': rglru_wx, 'rglru_bx': rglru_bx,
         'rglru_lambda': rglru_lambda, 'conf_dw_w': conf_dw_w, 'conf_dw_b': conf_dw_b,
         'conf_ln_g': conf_ln_g, 'conf_ln_b': conf_ln_b, 'ab_w_out': ab_w_out, 'ml_norm': ml_norm,
         'ml_w_in': ml_w_in, 'ml_b_if': ml_b_if, 'ml_hnorm_g': ml_hnorm_g, 'ml_w_out': ml_w_out,
         'mlp_norm': mlp_norm, 'mlp_w_up': mlp_w_up, 'mlp_w_down': mlp_w_down, 'final_norm': final_norm}
    W = _prep_weights(P)
    Bp, Tp, D = x_prompt.shape
    Bs, Ts, _ = x_sample.shape
    assert Ts == 1, "the sample group advances one token per sequence"
    states = (jnp.swapaxes(state_rglru_conv, 1, 2), state_rglru_h, jnp.swapaxes(state_conf_conv, 1, 2),
              state_mlstm_C, state_mlstm_n, state_mlstm_m)
    yp, ys, p_states, s_states = _run_trunk(x_prompt.reshape(Bp * Tp, D), x_sample.reshape(Bs, D),
                                            P, W, Bp, Tp, states)
    return (yp.reshape(Bp, Tp, D), ys.reshape(Bs, Ts, D), *p_states, *s_states)
```

```python
import functools
import math

import jax
import jax.numpy as jnp
from jax import lax
from jax.experimental import pallas as pl
from jax.experimental.pallas import tpu as pltpu

F32 = jnp.float32
BF16 = jnp.bfloat16
EPS = 1e-6
LRU_C = 8.0
ML_CHUNK = 128
GATE_BLOCK = 256
SUBLANES = 8
MIB = 1024 * 1024


def _cp(sem, vmem_mib):
    return pltpu.CompilerParams(dimension_semantics=sem, vmem_limit_bytes=vmem_mib * MIB)


def _tile(n, pref):
    return pref if n % pref == 0 else n


BF16_ROWS = 16
MM_ROWS_CAP = 1100
OUT_ROWS_CAP = 512


def _row_tile(m, cap):
    best = None
    for t in range(BF16_ROWS, min(m, cap) + 1, BF16_ROWS):
        if m % t == 0:
            best = t
    assert best is not None, (m, cap)
    return best


def _rms(x, g):
    return (x * lax.rsqrt(jnp.mean(x * x, axis=-1, keepdims=True) + EPS)) * g


def _sigmoid(x):
    return jax.nn.sigmoid(x)


def _softplus(z):
    return jnp.maximum(z, 0.0) + jnp.log1p(jnp.exp(-jnp.abs(z)))


def _log_sigmoid(x):
    return -_softplus(-x)


def _gelu_tanh(x):
    c = math.sqrt(2.0 / math.pi)
    return x * (0.5 * (1.0 + jnp.tanh(c * (x + 0.044715 * (x * x * x)))))


def _layernorm(x, g, b):
    mu = jnp.mean(x, axis=-1, keepdims=True)
    d = x - mu
    var = jnp.mean(d * d, axis=-1, keepdims=True)
    return (d * lax.rsqrt(var + EPS)) * g + b


def _dot(a, b):
    return jnp.dot(a, b, preferred_element_type=F32)


def _split3(x):
    hi = x.astype(BF16)
    r1 = x - hi.astype(F32)
    mid = r1.astype(BF16)
    lo = (r1 - mid.astype(F32)).astype(BF16)
    return hi, mid, lo


def _entry_kernel(*refs, has_acc):
    x_ref, g_ref = refs[:2]
    xo_ref, xn_ref = refs[2 + 2 * int(has_acc):]
    x = x_ref[...]
    xo_ref[...] = x
    xn_ref[...] = _rms(x, g_ref[...]).astype(xn_ref.dtype)


def entry_norm(x, g, m_all, row0, acc):
    M, D = x.shape
    tm = _tile(M, 512)
    assert row0 % tm == 0
    off = row0 // tm
    has_acc = acc is not None
    in_specs = [pl.BlockSpec((tm, D), lambda i: (i, 0)), pl.BlockSpec((1, D), lambda i: (0, 0))]
    args = [x, g.reshape(1, D)]
    if has_acc:
        in_specs += [pl.BlockSpec(memory_space=pl.ANY)] * 2
        args += list(acc)
    out_spec = pl.BlockSpec((tm, D), lambda i: (i + off, 0))
    return pl.pallas_call(
        functools.partial(_entry_kernel, has_acc=has_acc), grid=(M // tm,),
        in_specs=in_specs, out_specs=[out_spec, out_spec],
        out_shape=[jax.ShapeDtypeStruct((m_all, D), F32), jax.ShapeDtypeStruct((m_all, D), BF16)],
        input_output_aliases={2: 0, 3: 1} if has_acc else {},
        compiler_params=_cp(("parallel",), 32), name="entry_norm")(*args)


def _matmul_kernel(x_ref, w_ref, o_ref, wb_ref, *, transposed):
    @pl.when(pl.program_id(1) == 0)
    def _():
        wb_ref[...] = w_ref[...].astype(BF16)

    if transposed:
        y = lax.dot_general(x_ref[...], wb_ref[...], (((1,), (1,)), ((), ())), preferred_element_type=F32)
    else:
        y = _dot(x_ref[...], wb_ref[...])
    o_ref[...] = y.astype(o_ref.dtype)


def matmul(x, w, layer, n_cols, tn, transposed=False):
    M, K = x.shape
    tm = _row_tile(M, MM_ROWS_CAP)
    tn = _tile(n_cols, tn)
    if transposed:
        w_spec = pl.BlockSpec((None, tn, K), lambda j, i: (layer, j, 0))
        w_tile = (tn, K)
    else:
        w_spec = pl.BlockSpec((None, K, tn), lambda j, i: (layer, 0, j))
        w_tile = (K, tn)
    return pl.pallas_call(
        functools.partial(_matmul_kernel, transposed=transposed), grid=(n_cols // tn, M // tm),
        in_specs=[pl.BlockSpec((tm, K), lambda j, i: (i, 0)), w_spec],
        out_specs=pl.BlockSpec((tm, tn), lambda j, i: (i, j)),
        out_shape=jax.ShapeDtypeStruct((M, n_cols), F32),
        scratch_shapes=[pltpu.VMEM(w_tile, BF16)],
        compiler_params=_cp(("parallel", "arbitrary"), 48), name="in_proj")(x, w)


def _outproj_kernel(*refs, n_act):
    acts = refs[:n_act]
    w_ref, x_ref, g_ref, xo_ref, xn_ref = refs[n_act:]
    y = None
    k0 = 0
    for a_ref in acts:
        kw = a_ref.shape[1]
        part = _dot(a_ref[...].astype(BF16), w_ref[k0:k0 + kw, :])
        y = part if y is None else y + part
        k0 += kw
    x_new = x_ref[...] + y
    xo_ref[...] = x_new
    xn_ref[...] = _rms(x_new, g_ref[...]).astype(xn_ref.dtype)


def out_proj(acts, w, layer, x, g):
    M, D = x.shape
    K = w.shape[1]
    tm = _row_tile(M, OUT_ROWS_CAP)
    row = lambda i: (i, 0)
    fixed = lambda i: (0, 0)
    in_specs = [pl.BlockSpec((tm, a.shape[1]), row) for a in acts]
    in_specs += [pl.BlockSpec((None, K, D), lambda i: (layer, 0, 0), pipeline_mode=pl.Buffered(1)),
                 pl.BlockSpec((tm, D), row), pl.BlockSpec((1, D), fixed)]
    return pl.pallas_call(
        functools.partial(_outproj_kernel, n_act=len(acts)), grid=(M // tm,),
        in_specs=in_specs,
        out_specs=[pl.BlockSpec((tm, D), row), pl.BlockSpec((tm, D), row)],
        out_shape=[jax.ShapeDtypeStruct((M, D), F32), jax.ShapeDtypeStruct((M, D), BF16)],
        compiler_params=_cp(("parallel",), 48), name="out_proj")(*acts, w, x, g.reshape(1, D))


def _mlp_kernel(xn_ref, x_ref, wu_ref, wd_ref, g_ref, *rest, emit_x):
    xo_ref, xn_out_ref = rest if emit_x else rest[::-1]
    f = pl.program_id(1)

    @pl.when(f == 0)
    def _():
        xo_ref[...] = x_ref[...]

    h = _dot(xn_ref[...], wu_ref[...].astype(BF16))
    h = jnp.square(jnp.maximum(h, 0.0)).astype(BF16)
    xo_ref[...] += _dot(h, wd_ref[...].astype(BF16))

    @pl.when(f == pl.num_programs(1) - 1)
    def _():
        xn_out_ref[...] = _rms(xo_ref[...], g_ref[...]).astype(xn_out_ref.dtype)


def mlp(xn, x, w_up, w_down, layer, g, out_dtype, row0=0, nrows=None, emit_x=True):
    D = x.shape[1]
    M = x.shape[0] if nrows is None else nrows
    F = w_up.shape[2]
    tm = _row_tile(M, MM_ROWS_CAP)
    assert row0 % tm == 0
    off = row0 // tm
    tf = _tile(F, 512)
    row_in = lambda i, f: (i + off, 0)
    row = lambda i, f: (i, 0)
    once = pl.Buffered(1)
    row_out = pl.BlockSpec((tm, D), row, pipeline_mode=once)
    xn_shape = jax.ShapeDtypeStruct((M, D), out_dtype)
    outs = pl.pallas_call(
        functools.partial(_mlp_kernel, emit_x=emit_x), grid=(M // tm, F // tf),
        in_specs=[pl.BlockSpec((tm, D), row_in), pl.BlockSpec((tm, D), row_in, pipeline_mode=once),
                  pl.BlockSpec((None, D, tf), lambda i, f: (layer, 0, f)),
                  pl.BlockSpec((None, tf, D), lambda i, f: (layer, f, 0)),
                  pl.BlockSpec((1, D), lambda i, f: (0, 0))],
        out_specs=[row_out, row_out] if emit_x else [row_out],
        out_shape=[jax.ShapeDtypeStruct((M, D), F32), xn_shape] if emit_x else [xn_shape],
        scratch_shapes=[] if emit_x else [pltpu.VMEM((tm, D), F32)],
        compiler_params=_cp(("parallel", "arbitrary"), 56), name="mlp")(xn, x, w_up, w_down, g.reshape(1, D))
    return outs if emit_x else (None, outs[0])


def _rglru_gates(xc, wg_ref, ba, bx, lam):
    C = xc.shape[1]
    xcb = xc.astype(BF16)
    r_parts, i_parts = [], []
    for c in range(C // GATE_BLOCK):
        lo, hi = c * GATE_BLOCK, (c + 1) * GATE_BLOCK
        gates = _dot(xcb[:, lo:hi], wg_ref[c])
        r_parts.append(gates[:, :GATE_BLOCK])
        i_parts.append(gates[:, GATE_BLOCK:])
    r = _sigmoid(jnp.concatenate(r_parts, axis=1) + ba)
    ig = _sigmoid(jnp.concatenate(i_parts, axis=1) + bx)
    log_a = (-LRU_C * _softplus(-lam)) * r
    a = jnp.exp(log_a)
    u = jnp.sqrt(-jnp.tanh(log_a) * (a * a + 1.0)) * (ig * xc)
    return a, u


def _rglru_prompt_kernel(xr_ref, gr_ref, cw_ref, cb_ref, wg_ref, ba_ref, bx_ref, lam_ref,
                         y_ref, tail_ref, hlast_ref, xp_ref, a_ref, u_ref, h_ref):
    t = pl.program_id(1)
    tT, C = xr_ref.shape
    nk = cw_ref.shape[0]

    @pl.when(t == 0)
    def _():
        xp_ref[0:SUBLANES, :] = jnp.zeros((SUBLANES, C), F32)
        h_ref[...] = jnp.zeros_like(h_ref)

    x = xr_ref[...]
    xp_ref[SUBLANES:SUBLANES + tT, :] = x
    xc = cb_ref[...] + cw_ref[nk - 1:nk, :] * x
    for k in range(nk - 1):
        off = SUBLANES - (nk - 1) + k
        xc = xc + cw_ref[k:k + 1, :] * xp_ref[off:off + tT, :]
    xp_ref[0:SUBLANES, :] = x[tT - SUBLANES:tT, :]

    a, u = _rglru_gates(xc, wg_ref, ba_ref[...], bx_ref[...], lam_ref[...])
    a_ref[...] = a
    u_ref[...] = u

    rowi = lax.broadcasted_iota(jnp.int32, (SUBLANES, C), 0)

    def body(g, h):
        r0 = pl.multiple_of(g * SUBLANES, SUBLANES)
        a8 = a_ref[pl.ds(r0, SUBLANES), :]
        u8 = u_ref[pl.ds(r0, SUBLANES), :]
        for s in (1, 2, 4):
            a_sh = pltpu.roll(a8, s, 0)
            u_sh = pltpu.roll(u8, s, 0)
            m = rowi >= s
            u8 = jnp.where(m, a8 * u_sh + u8, u8)
            a8 = jnp.where(m, a8 * a_sh, a8)
        h8 = a8 * h + u8
        u_ref[pl.ds(r0, SUBLANES), :] = h8
        return h8[SUBLANES - 1:SUBLANES, :]

    h = lax.fori_loop(0, tT // SUBLANES, body, h_ref[...])
    h_ref[...] = h
    y_ref[...] = (u_ref[...] * _gelu_tanh(gr_ref[...])).astype(y_ref.dtype)

    @pl.when(t == pl.num_programs(1) - 1)
    def _():
        tail_ref[0] = x[tT - SUBLANES:tT, :]
        hlast_ref[0] = h


def rglru_prompt(proj, B, T, cw, cb, wg, ba, bx, lam):
    C = cw.shape[1]
    tT = _tile(T, 256)
    nT = T // tT
    fixed2 = lambda b, t: (0, 0)
    vec = pl.BlockSpec((1, C), fixed2)
    return pl.pallas_call(
        _rglru_prompt_kernel, grid=(B, nT),
        in_specs=[pl.BlockSpec((tT, C), lambda b, t: (b * nT + t, 0)),
                  pl.BlockSpec((tT, C), lambda b, t: (b * nT + t, 1)),
                  pl.BlockSpec(cw.shape, fixed2), vec,
                  pl.BlockSpec(wg.shape, lambda b, t: (0, 0, 0)), vec, vec, vec],
        out_specs=[pl.BlockSpec((tT, C), lambda b, t: (b * nT + t, 0)),
                   pl.BlockSpec((1, SUBLANES, C), lambda b, t: (b, 0, 0)),
                   pl.BlockSpec((1, 1, C), lambda b, t: (b, 0, 0))],
        out_shape=[jax.ShapeDtypeStruct((proj.shape[0], C), BF16),
                   jax.ShapeDtypeStruct((B, SUBLANES, C), F32),
                   jax.ShapeDtypeStruct((B, 1, C), F32)],
        scratch_shapes=[pltpu.VMEM((SUBLANES + tT, C), F32), pltpu.VMEM((tT, C), F32),
                        pltpu.VMEM((tT, C), F32), pltpu.VMEM((1, C), F32)],
        compiler_params=_cp(("parallel", "arbitrary"), 40), name="rglru_prompt",
    )(proj, proj, cw, cb.reshape(1, C), wg, ba.reshape(1, C), bx.reshape(1, C), lam.reshape(1, C))


CONF_STRIDE = 4
LANES = 128


def _conf_prompt_kernel(ac_ref, bc_ref, dw_ref, db_ref, lg_ref, lb_ref,
                        y_ref, tail_ref, gp_ref, yc_ref):
    t = pl.program_id(1)
    tT, C = ac_ref.shape
    nk = dw_ref.shape[0]
    pad = gp_ref.shape[1] - tT
    nslab = C // LANES
    span = CONF_STRIDE * SUBLANES

    @pl.when(t == 0)
    def _():
        gp_ref[:, 0:pad, :] = jnp.zeros((nslab, pad, LANES), F32)

    glu = ac_ref[...] * _sigmoid(bc_ref[...])
    for c in range(nslab):
        gp_ref[c, pad:pad + tT, :] = glu[:, c * LANES:(c + 1) * LANES]
    for c in range(nslab):
        lo = c * LANES
        for r0 in range(0, tT, span):
            accs = [jnp.broadcast_to(db_ref[:, lo:lo + LANES], (SUBLANES, LANES)) for _ in range(CONF_STRIDE)]
            for k in range(nk):
                w = dw_ref[k:k + 1, lo:lo + LANES]
                off = pad - (nk - 1) + k + r0
                for p in range(CONF_STRIDE):
                    accs[p] = accs[p] + gp_ref[c, pl.ds(off + p, SUBLANES, stride=CONF_STRIDE), :] * w
            for p in range(CONF_STRIDE):
                yc_ref[c, pl.ds(r0 + p, SUBLANES, stride=CONF_STRIDE), :] = accs[p]
    yc = jnp.concatenate([yc_ref[c] for c in range(nslab)], axis=1)
    yn = _layernorm(yc, lg_ref[...], lb_ref[...])
    y_ref[...] = (yn * _sigmoid(yn)).astype(y_ref.dtype)

    @pl.when(t == pl.num_programs(1) - 1)
    def _():
        tail_ref[0] = jnp.concatenate([gp_ref[c, tT:tT + pad, :] for c in range(nslab)], axis=1)

    gp_ref[:, 0:pad, :] = gp_ref[:, tT:tT + pad, :]


def conf_prompt(proj, B, T, dw, db, lg, lb):
    nk, C = dw.shape
    tT = _tile(T, 128)
    nT = T // tT
    pad = -(-(nk - 1) // SUBLANES) * SUBLANES
    assert tT % (CONF_STRIDE * SUBLANES) == 0 and C % LANES == 0
    fixed2 = lambda b, t: (0, 0)
    vec = pl.BlockSpec((1, C), fixed2)
    return pl.pallas_call(
        _conf_prompt_kernel, grid=(B, nT),
        in_specs=[pl.BlockSpec((tT, C), lambda b, t: (b * nT + t, 2)),
                  pl.BlockSpec((tT, C), lambda b, t: (b * nT + t, 3)),
                  pl.BlockSpec((nk, C), fixed2), vec, vec, vec],
        out_specs=[pl.BlockSpec((tT, C), lambda b, t: (b * nT + t, 0)),
                   pl.BlockSpec((1, pad, C), lambda b, t: (b, 0, 0))],
        out_shape=[jax.ShapeDtypeStruct((proj.shape[0], C), BF16), jax.ShapeDtypeStruct((B, pad, C), F32)],
        scratch_shapes=[pltpu.VMEM((C // LANES, pad + tT, LANES), F32),
                        pltpu.VMEM((C // LANES, tT, LANES), F32)],
        compiler_params=_cp(("parallel", "arbitrary"), 32), name="conf_prompt",
    )(proj, proj, dw, db.reshape(1, C), lg.reshape(1, C), lb.reshape(1, C))


def _ab_sample_kernel(*refs, has_acc):
    (proj_ref, rconv_ref, h0_ref, cconv_ref,
     cw_ref, cb_ref, wg_ref, ba_ref, bx_ref, lam_ref, dw_ref, db_ref, lg_ref, lb_ref) = refs[:14]
    yr_ref, yc_ref, rconv_o, h_o, cconv_o = refs[14 + int(has_acc):]
    C = h0_ref.shape[1]
    nk = cw_ref.shape[0]
    nd = dw_ref.shape[0]
    x_r = proj_ref[:, 0:C]
    g_r = proj_ref[:, C:2 * C]
    glu = proj_ref[:, 2 * C:3 * C] * _sigmoid(proj_ref[:, 3 * C:4 * C])

    xc = cb_ref[...] + cw_ref[nk - 1:nk, :] * x_r
    for k in range(nk - 1):
        xc = xc + cw_ref[k:k + 1, :] * rconv_ref[k]
    for k in range(nk - 2):
        rconv_o[k] = rconv_ref[k + 1]
    rconv_o[nk - 2] = x_r

    yc = db_ref[...] + dw_ref[nd - 1:nd, :] * glu
    for k in range(nd - 1):
        yc = yc + dw_ref[k:k + 1, :] * cconv_ref[k]
    for k in range(nd - 2):
        cconv_o[k] = cconv_ref[k + 1]
    cconv_o[nd - 2] = glu

    a, u = _rglru_gates(xc, wg_ref, ba_ref[...], bx_ref[...], lam_ref[...])
    h = a * h0_ref[...] + u
    h_o[...] = h
    yr_ref[...] = h * _gelu_tanh(g_r)
    yn = _layernorm(yc, lg_ref[...], lb_ref[...])
    yc_ref[...] = yn * _sigmoid(yn)


def ab_sample(proj, row0, rconv_t, h0, cconv_t, layer, cconv_acc, cw, cb, wg, ba, bx, lam, dw, db, lg, lb):
    _, B, C = h0.shape
    bt = _tile(B, 32)
    row = lambda i: (i, 0)
    fixed = lambda i: (0, 0)
    vec = pl.BlockSpec((1, C), fixed)
    nr, nc = rconv_t.shape[1], cconv_t.shape[1]
    st_c = pl.BlockSpec((None, nc, bt, C), lambda i: (layer, 0, i, 0))
    act = pl.BlockSpec((bt, C), row)
    has_acc = cconv_acc is not None
    assert row0 % bt == 0
    off = row0 // bt
    in_specs = [pl.BlockSpec((bt, proj.shape[1]), lambda i: (i + off, 0)),
                pl.BlockSpec((None, nr, bt, C), lambda i: (layer, 0, i, 0)),
                pl.BlockSpec((None, bt, C), lambda i: (layer, i, 0)), st_c,
                pl.BlockSpec(cw.shape, fixed), vec, pl.BlockSpec(wg.shape, lambda i: (0, 0, 0)),
                vec, vec, vec, pl.BlockSpec(dw.shape, fixed), vec, vec, vec]
    args = [proj, rconv_t, h0, cconv_t, cw, cb.reshape(1, C), wg, ba.reshape(1, C), bx.reshape(1, C),
            lam.reshape(1, C), dw, db.reshape(1, C), lg.reshape(1, C), lb.reshape(1, C)]
    if has_acc:
        in_specs.append(pl.BlockSpec(memory_space=pl.ANY))
        args.append(cconv_acc)
    return pl.pallas_call(
        functools.partial(_ab_sample_kernel, has_acc=has_acc), grid=(B // bt,),
        in_specs=in_specs,
        out_specs=[act, act, pl.BlockSpec((nr, bt, C), lambda i: (0, i, 0)), act, st_c],
        out_shape=[jax.ShapeDtypeStruct((B, C), F32), jax.ShapeDtypeStruct((B, C), F32),
                   jax.ShapeDtypeStruct((nr, B, C), F32), jax.ShapeDtypeStruct((B, C), F32),
                   jax.ShapeDtypeStruct(cconv_t.shape, F32)],
        input_output_aliases={len(args) - 1: 4} if has_acc else {},
        compiler_params=_cp(("parallel",), 40), name="ab_sample",
    )(*args)


def _mlstm_prompt_kernel(q_ref, k_ref, v_ref, o_ref, g_ref, bif_ref, hng_ref,
                         y_ref, C_ref, n_ref, m_ref, *, heads):
    c = pl.program_id(1)
    L = q_ref.shape[0]
    H = heads
    DK = q_ref.shape[1] // H
    DV = v_ref.shape[1] // H
    scale = DK ** -0.5

    @pl.when(c == 0)
    def _():
        C_ref[...] = jnp.zeros_like(C_ref)
        n_ref[...] = jnp.zeros_like(n_ref)
        m_ref[...] = jnp.zeros_like(m_ref)

    ti = lax.broadcasted_iota(jnp.int32, (L, L), 0)
    si = lax.broadcasted_iota(jnp.int32, (L, L), 1)
    causal = si <= ti
    tri_lo = causal.astype(BF16)
    tri_up = (ti <= si).astype(BF16)

    gates = g_ref[...] + bif_ref[...]
    lf = _log_sigmoid(gates)
    gates_t = gates.T
    lf_t = lf.T
    hi, mid, lo = _split3(lf)
    b_cols = (_dot(tri_lo, lo) + _dot(tri_lo, mid)) + _dot(tri_lo, hi)
    hi, mid, lo = _split3(lf_t)
    b_rows = (_dot(lo, tri_up) + _dot(mid, tri_up)) + _dot(hi, tri_up)

    m_all = m_ref[0]
    n_all = n_ref[0]
    hs = range(H)
    dk = lambda h: slice(h * DK, (h + 1) * DK)
    dv = lambda h: slice(h * DV, (h + 1) * DV)
    ig_col = [gates[:, h:h + 1] for h in hs]
    ig_row = [gates_t[h:h + 1, :] for h in hs]
    b_col = [b_cols[:, H + h:H + h + 1] for h in hs]
    b_row = [b_rows[H + h:H + h + 1, :] for h in hs]
    m_prev = [m_all[h:h + 1, 0:1] for h in hs]
    n_row = [n_all[h:h + 1, :] for h in hs]

    qs = [q_ref[:, dk(h)] * scale for h in hs]
    qb = [qs[h].astype(BF16) for h in hs]
    kb = [k_ref[:, dk(h)].astype(BF16) for h in hs]
    vb = [v_ref[:, dv(h)].astype(BF16) for h in hs]
    dmat = [jnp.where(causal, b_col[h] - b_row[h] + ig_row[h], -jnp.inf) for h in hs]
    inter = [b_col[h] + m_prev[h] for h in hs]
    m_t = [jnp.maximum(inter[h], jnp.max(dmat[h], axis=-1, keepdims=True)) for h in hs]
    qk = [lax.dot_general(qb[h], kb[h], (((1,), (1,)), ((), ())), preferred_element_type=F32) for h in hs]
    s = [qk[h] * jnp.exp(dmat[h] - m_t[h]) for h in hs]
    scale_in = [jnp.exp(inter[h] - m_t[h]) for h in hs]
    qC = [_dot(qb[h], C_ref[0, h].astype(BF16)) for h in hs]
    num = [_dot(s[h].astype(BF16), vb[h]) + scale_in[h] * qC[h] for h in hs]
    den = [jnp.sum(s[h], axis=-1, keepdims=True)
           + scale_in[h] * jnp.sum(qs[h] * n_row[h], axis=-1, keepdims=True) for h in hs]
    hh = [num[h] / jnp.maximum(jnp.abs(den[h]), jnp.exp(-m_t[h])) for h in hs]
    b_last = [b_col[h][L - 1:L, :] for h in hs]
    g_row = [b_last[h] - b_row[h] + ig_row[h] for h in hs]
    g_col = [b_last[h] - b_col[h] + ig_col[h] for h in hs]
    m_new = [jnp.maximum(b_last[h] + m_prev[h], jnp.max(g_row[h], axis=-1, keepdims=True)) for h in hs]
    decay = [jnp.exp(b_last[h] + m_prev[h] - m_new[h]) for h in hs]
    kw = [k_ref[:, dk(h)] * jnp.exp(g_col[h] - m_new[h]) for h in hs]
    for h in hs:
        C_ref[0, h] = decay[h] * C_ref[0, h] + lax.dot_general(
            kw[h].astype(BF16), vb[h], (((0,), (0,)), ((), ())), preferred_element_type=F32)
    n_ref[0] = jnp.concatenate([decay[h] * n_row[h] + jnp.sum(kw[h], axis=0, keepdims=True) for h in hs], axis=0)
    m_ref[0] = jnp.concatenate([jnp.broadcast_to(m_new[h], (1, m_ref.shape[2])) for h in hs], axis=0)
    for h in hs:
        hn = hh[h] * lax.rsqrt(jnp.mean(hh[h] * hh[h], axis=-1, keepdims=True) + EPS)
        y = (hn * hng_ref[:, dv(h)]) * _sigmoid(o_ref[:, dv(h)])
        y_ref[:, dv(h)] = y.astype(y_ref.dtype)


def mlstm_prompt(proj, gates, bif_row, hng, B, T, heads):
    QK = proj.shape[1] // 6
    V = 2 * QK
    H = heads
    DK, DV = QK // H, V // H
    L = ML_CHUNK if T % ML_CHUNK == 0 else T
    nc = T // L
    rowmap = lambda j: (lambda b, c: (b * nc + c, j))
    fixed2 = lambda b, c: (0, 0)
    return pl.pallas_call(
        functools.partial(_mlstm_prompt_kernel, heads=H), grid=(B, nc),
        in_specs=[pl.BlockSpec((L, QK), rowmap(0)), pl.BlockSpec((L, QK), rowmap(1)),
                  pl.BlockSpec((L, V), rowmap(1)), pl.BlockSpec((L, V), rowmap(2)),
                  pl.BlockSpec((L, LANES), rowmap(0)), pl.BlockSpec((1, LANES), fixed2),
                  pl.BlockSpec((1, V), fixed2)],
        out_specs=[pl.BlockSpec((L, V), rowmap(0)),
                   pl.BlockSpec((1, H, DK, DV), lambda b, c: (b, 0, 0, 0)),
                   pl.BlockSpec((1, H, DK), lambda b, c: (b, 0, 0)),
                   pl.BlockSpec((1, H, LANES), lambda b, c: (b, 0, 0))],
        out_shape=[jax.ShapeDtypeStruct((proj.shape[0], V), BF16),
                   jax.ShapeDtypeStruct((B, H, DK, DV), F32),
                   jax.ShapeDtypeStruct((B, H, DK), F32),
                   jax.ShapeDtypeStruct((B, H, LANES), F32)],
        compiler_params=_cp(("parallel", "arbitrary"), 40), name="mlstm_prompt",
    )(proj, proj, proj, proj, gates, bif_row, hng.reshape(1, V))


def _mlstm_gate_terms(ig, fg, m):
    lf = _log_sigmoid(fg)
    inter = lf + m
    m_t = jnp.maximum(inter, ig)
    return m_t, jnp.exp(ig - m_t), jnp.exp(inter - m_t)


def _mlstm_sample_kernel(*refs, has_acc):
    (qt_ref, kt_ref, qr_ref, kr_ref, v_ref, o_ref, gr_ref, gc_ref, bifr_ref, bifc_ref,
     hng_ref, C_ref, n_ref) = refs[:13]
    y_ref, Co_ref, no_ref, mo_ref = refs[13 + int(has_acc):]
    bb, H, DK = qr_ref.shape
    scale = DK ** -0.5
    bifr = bifr_ref[...]
    bifc = bifc_ref[...]
    for j in range(bb):
        gr = gr_ref[j]
        gc = gc_ref[j]
        mt_r, es_r, sc_r = _mlstm_gate_terms(gr[:, 0:H] + bifr[:, 0:H], gr[:, H:2 * H] + bifr[:, H:2 * H],
                                             gr[:, 2 * H:3 * H])
        mt_c, es_c, sc_c = _mlstm_gate_terms(gc[0:H, :] + bifc[0:H, :], gc[H:2 * H, :] + bifc[H:2 * H, :],
                                             gc[2 * H:3 * H, :])
        qr = qr_ref[j] * scale
        kr = kr_ref[j]
        n = n_ref[j]
        v = v_ref[j]
        qk = jnp.sum(qr * kr, axis=1, keepdims=True)
        qn = jnp.sum(qr * n, axis=1, keepdims=True)
        qt = qt_ref[j] * scale
        kts = kt_ref[j] * es_r
        rows = []
        for h in range(H):
            Ch = C_ref[j, h]
            rows.append(jnp.sum(Ch * qt[:, h:h + 1], axis=0, keepdims=True))
            Co_ref[j, h] = sc_r[:, h:h + 1] * Ch + kts[:, h:h + 1] * v[h:h + 1, :]
        qC = jnp.concatenate(rows, axis=0)
        s = qk * es_c
        num = s * v + sc_c * qC
        den = s + sc_c * qn
        hh = num / jnp.maximum(jnp.abs(den), jnp.exp(-mt_c))
        hn = hh * lax.rsqrt(jnp.mean(hh * hh, axis=-1, keepdims=True) + EPS)
        y_ref[j] = (hn * hng_ref[...]) * _sigmoid(o_ref[j])
        no_ref[j] = sc_c * n + es_c * kr
        mo_ref[j] = mt_r


def mlstm_sample(q, k, v, o, gates, m, bif, hng, C, n, layer, C_acc):
    B, H, DK = q.shape
    DV = v.shape[2]
    bb = _tile(B, 8)
    gm = jnp.concatenate([gates, m], axis=1)
    qt = jnp.swapaxes(q, 1, 2)
    kt = jnp.swapaxes(k, 1, 2)
    b3 = lambda i: (i, 0, 0)
    fixed = lambda i: (0, 0)
    blk = lambda a: pl.BlockSpec((bb,) + a.shape[1:], b3)
    C_spec = pl.BlockSpec((None, bb, H, DK, DV), lambda i: (layer, i, 0, 0, 0))
    has_acc = C_acc is not None
    in_specs = [blk(qt), blk(kt), blk(q), blk(k), blk(v), blk(o),
                pl.BlockSpec((bb, 1, 3 * H), b3), pl.BlockSpec((bb, 3 * H, 1), b3),
                pl.BlockSpec((1, 2 * H), fixed), pl.BlockSpec((2 * H, 1), fixed),
                pl.BlockSpec((H, DV), fixed), C_spec,
                pl.BlockSpec((None, bb, H, DK), lambda i: (layer, i, 0, 0))]
    args = [qt, kt, q, k, v, o, gm.reshape(B, 1, 3 * H), gm.reshape(B, 3 * H, 1),
            bif.reshape(1, 2 * H), bif.reshape(2 * H, 1), hng.reshape(H, DV), C, n]
    if has_acc:
        in_specs.append(pl.BlockSpec(memory_space=pl.ANY))
        args.append(C_acc)
    return pl.pallas_call(
        functools.partial(_mlstm_sample_kernel, has_acc=has_acc), grid=(B // bb,),
        in_specs=in_specs,
        out_specs=[blk(v), C_spec, pl.BlockSpec((bb, H, DK), b3), pl.BlockSpec((bb, 1, H), b3)],
        out_shape=[jax.ShapeDtypeStruct((B, H, DV), F32), jax.ShapeDtypeStruct(C.shape, F32),
                   jax.ShapeDtypeStruct((B, H, DK), F32), jax.ShapeDtypeStruct((B, 1, H), F32)],
        input_output_aliases={len(args) - 1: 1} if has_acc else {},
        compiler_params=_cp(("parallel",), 48), name="mlstm_sample",
    )(*args)


def _blockdiag(w, per):
    H, d, _ = w.shape
    w4 = w.reshape(H // per, per, d, d)
    eye = jnp.eye(per, dtype=w.dtype)
    return (w4[:, :, :, None, :] * eye[None, :, None, :, None]).reshape(H // per, per * d, per * d)


def _prep_weights(P):
    W = {}
    W['ab_w_out'] = P['ab_w_out'].astype(BF16)
    W['ml_w_out'] = P['ml_w_out'].astype(BF16)
    heads = P['ml_b_if'].shape[1] // 2
    n_main = P['ml_w_in'].shape[2] - 2 * heads
    W['ml_w_in_t'] = jnp.swapaxes(P['ml_w_in'], 1, 2)
    W['ml_w_gate'] = jnp.pad(P['ml_w_in'][:, :, n_main:], ((0, 0), (0, 0), (0, LANES - 2 * heads)))
    W['ml_bif_row'] = jnp.pad(P['ml_b_if'], ((0, 0), (0, LANES - 2 * heads)))[:, None, :]
    d = P['rglru_wa'].shape[2]
    per = GATE_BLOCK // d
    wa = jax.vmap(lambda w: _blockdiag(w, per))(P['rglru_wa'])
    wx = jax.vmap(lambda w: _blockdiag(w, per))(P['rglru_wx'])
    W['rglru_wg'] = jnp.concatenate([wa, wx], axis=-1).astype(BF16)
    return W


def _set_rows(y_all, y_rows, row0):
    return lax.dynamic_update_slice(y_all, y_rows.astype(y_all.dtype), (row0, 0))


def _run_trunk(xp, xs, P, W, Bp, Tp, states):
    depth = P['mlp_norm'].shape[0]
    heads = P['ml_b_if'].shape[1] // 2
    n_main = P['ml_w_in'].shape[2] - 2 * heads
    Mp, Bs = xp.shape[0], xs.shape[0]
    m_all = Mp + Bs
    p_rconv, p_rh, p_cconv, p_C, p_n, p_m = [], [], [], [], [], []
    s_rconv, s_rh, s_n, s_m = [], [], [], []
    cconv_acc = None
    C_acc = None
    x, xn = entry_norm(xs, P['ab_norm'][0], m_all, Mp, entry_norm(xp, P['ab_norm'][0], m_all, 0, None))
    for l in range(depth):
        i = l // 2
        if l % 2 == 0:
            proj = matmul(xn, P['ab_w_in'], i, P['ab_w_in'].shape[2], 1024)
            rg = (P['rglru_conv_w'][i], P['rglru_conv_b'][i], W['rglru_wg'][i], P['rglru_ba'][i],
                  P['rglru_bx'][i], P['rglru_lambda'][i])
            cf = (P['conf_dw_w'][i], P['conf_dw_b'][i], P['conf_ln_g'][i], P['conf_ln_b'][i])
            y_r, tail_r, h_last = rglru_prompt(proj, Bp, Tp, *rg)
            y_c, tail_c = conf_prompt(proj, Bp, Tp, *cf)
            nr = P['rglru_conv_w'].shape[1] - 1
            ncf = P['conf_dw_w'].shape[1] - 1
            p_rconv.append(tail_r[:, tail_r.shape[1] - nr:])
            p_rh.append(h_last[:, 0])
            p_cconv.append(tail_c[:, tail_c.shape[1] - ncf:])
            ys_r, ys_c, rconv_new, h_new, cconv_acc = ab_sample(proj, Mp, states[0], states[1], states[2], i,
                                                                cconv_acc, *rg, *cf)
            s_rconv.append(rconv_new); s_rh.append(h_new)
            acts = [_set_rows(y_r, ys_r, Mp), _set_rows(y_c, ys_c, Mp)]
            x, xn = out_proj(acts, W['ab_w_out'], i, x, P['mlp_norm'][l])
        else:
            proj = matmul(xn, W['ml_w_in_t'], i, n_main, 1024, transposed=True)
            gates = matmul(xn, W['ml_w_gate'], i, LANES, LANES)
            hng = P['ml_hnorm_g'][i]
            y, C, n, m = mlstm_prompt(proj, gates, W['ml_bif_row'][i], hng, Bp, Tp, heads)
            p_C.append(C); p_n.append(n); p_m.append(m[:, :, 0])
            QK = n_main // 6
            V = 2 * QK
            ps = proj[Mp:]
            q = ps[:, :QK].reshape(Bs, heads, QK // heads)
            k = ps[:, QK:2 * QK].reshape(Bs, heads, QK // heads)
            v = ps[:, 2 * QK:2 * QK + V].reshape(Bs, heads, V // heads)
            o = ps[:, 2 * QK + V:].reshape(Bs, heads, V // heads)
            ys, C_acc, n, m = mlstm_sample(q, k, v, o, gates[Mp:, :2 * heads], states[5][i], P['ml_b_if'][i],
                                           hng, states[3], states[4], i, C_acc)
            s_n.append(n); s_m.append(m[:, 0])
            x, xn = out_proj([_set_rows(y, ys.reshape(Bs, V), Mp)], W['ml_w_out'], i, x, P['mlp_norm'][l])
        if l < depth - 1:
            g_next = P['ab_norm'][(l + 1) // 2] if (l + 1) % 2 == 0 else P['ml_norm'][(l + 1) // 2]
            x, xn = mlp(xn, x, P['mlp_w_up'], P['mlp_w_down'], l, g_next, BF16)
    last = depth - 1
    _, yp = mlp(xn, x, P['mlp_w_up'], P['mlp_w_down'], last, P['final_norm'], F32, 0, Mp, emit_x=False)
    _, ys = mlp(xn, x, P['mlp_w_up'], P['mlp_w_down'], last, P['final_norm'], F32, Mp, Bs, emit_x=False)
    prompt_out = (jnp.stack(p_rconv), jnp.stack(p_rh), jnp.stack(p_cconv),
                  jnp.stack(p_C), jnp.stack(p_n), jnp.stack(p_m))
    sample_out = (jnp.swapaxes(jnp.stack(s_rconv), 1, 2), jnp.stack(s_rh), jnp.swapaxes(cconv_acc, 1, 2),
                  C_acc, jnp.stack(s_n), jnp.stack(s_m))
    return yp, ys, prompt_out, sample_out


def kernel(x_prompt, x_sample, state_rglru_conv, state_rglru_h, state_conf_conv, state_mlstm_C, state_mlstm_n, state_mlstm_m, ab_norm, ab_w_in, rglru_conv_w, rglru_conv_b, rglru_wa, rglru_ba, rglru_wx, rglru_bx, rglru_lambda, conf_dw_w, conf_dw_b, conf_ln_g, conf_ln_b, ab_w_out, ml_norm, ml_w_in, ml_b_if, ml_hnorm_g, ml_w_out, mlp_norm, mlp_w_up, mlp_w_down, final_norm):
    P = {'ab_norm': ab_norm, 'ab_w_in': ab_w_in, 'rglru_conv_w': rglru_conv_w, 'rglru_conv_b': rglru_conv_b,
         'rglru_wa': rglru_wa, 'rglru_ba': rglru_ba, 'rglru_wx': rglru_wx, 'rglru_bx': rglru_bx,
         'rglru_lambda': rglru_lambda, 'conf_dw_w': conf_dw_w, 'conf_dw_b': conf_dw_b,
         'conf_ln_g': conf_ln_g, 'conf_ln_b': conf_ln_b, 'ab_w_out': ab_w_out, 'ml_norm': ml_norm,
         'ml_w_in': ml_w_in, 'ml_b_if': ml_b_if, 'ml_hnorm_g': ml_hnorm_g, 'ml_w_out': ml_w_out,
         'mlp_norm': mlp_norm, 'mlp_w_up': mlp_w_up, 'mlp_w_down': mlp_w_down, 'final_norm': final_norm}
    W = _prep_weights(P)
    Bp, Tp, D = x_prompt.shape
    Bs, Ts, _ = x_sample.shape
    assert Ts == 1, "the sample group advances one token per sequence"
    states = (jnp.swapaxes(state_rglru_conv, 1, 2), state_rglru_h, jnp.swapaxes(state_conf_conv, 1, 2),
              state_mlstm_C, state_mlstm_n, state_mlstm_m)
    yp, ys, p_states, s_states = _run_trunk(x_prompt.reshape(Bp * Tp, D), x_sample.reshape(Bs, D),
                                            P, W, Bp, Tp, states)
    return (yp.reshape(Bp, Tp, D), ys.reshape(Bs, Ts, D), *p_states, *s_states)
```

```python
import functools
import math

import jax
import jax.numpy as jnp
from jax import lax
from jax.experimental import pallas as pl
from jax.experimental.pallas import tpu as pltpu

F32 = jnp.float32
BF16 = jnp.bfloat16
EPS = 1e-6
LRU_C = 8.0
ML_CHUNK = 128
GATE_BLOCK = 256
SUBLANES = 8
MIB = 1024 * 1024


def _cp(sem, vmem_mib):
    return pltpu.CompilerParams(dimension_semantics=sem, vmem_limit_bytes=vmem_mib * MIB)


def _tile(n, pref):
    return pref if n % pref == 0 else n


BF16_ROWS = 16
MM_ROWS_CAP = 1100
MLP_TF_F32 = 512
MLP_TF_BF16 = 1024
MLP_VMEM_MIB = 56
OUT_ROWS_CAP = 640


def _row_tile(m, cap):
    best = None
    for t in range(BF16_ROWS, min(m, cap) + 1, BF16_ROWS):
        if m % t == 0:
            best = t
    assert best is not None, (m, cap)
    return best


def _rms(x, g):
    return (x * lax.rsqrt(jnp.mean(x * x, axis=-1, keepdims=True) + EPS)) * g


def _sigmoid(x):
    return jax.nn.sigmoid(x)


def _softplus(z):
    return jnp.maximum(z, 0.0) + jnp.log1p(jnp.exp(-jnp.abs(z)))


def _log_sigmoid(x):
    return -_softplus(-x)


def _gelu_tanh(x):
    c = math.sqrt(2.0 / math.pi)
    return x * (0.5 * (1.0 + jnp.tanh(c * (x + 0.044715 * (x * x * x)))))


def _layernorm(x, g, b):
    mu = jnp.mean(x, axis=-1, keepdims=True)
    d = x - mu
    var = jnp.mean(d * d, axis=-1, keepdims=True)
    return (d * lax.rsqrt(var + EPS)) * g + b


def _dot(a, b):
    return jnp.dot(a, b, preferred_element_type=F32)


def _split3(x):
    hi = x.astype(BF16)
    r1 = x - hi.astype(F32)
    mid = r1.astype(BF16)
    lo = (r1 - mid.astype(F32)).astype(BF16)
    return hi, mid, lo


def _entry_kernel(*refs, has_acc):
    x_ref, g_ref = refs[:2]
    xo_ref, xn_ref = refs[2 + 2 * int(has_acc):]
    x = x_ref[...]
    xo_ref[...] = x
    xn_ref[...] = _rms(x, g_ref[...]).astype(xn_ref.dtype)


def entry_norm(x, g, m_all, row0, acc):
    M, D = x.shape
    tm = _tile(M, 512)
    assert row0 % tm == 0
    off = row0 // tm
    has_acc = acc is not None
    in_specs = [pl.BlockSpec((tm, D), lambda i: (i, 0)), pl.BlockSpec((1, D), lambda i: (0, 0))]
    args = [x, g.reshape(1, D)]
    if has_acc:
        in_specs += [pl.BlockSpec(memory_space=pl.ANY)] * 2
        args += list(acc)
    out_spec = pl.BlockSpec((tm, D), lambda i: (i + off, 0))
    return pl.pallas_call(
        functools.partial(_entry_kernel, has_acc=has_acc), grid=(M // tm,),
        in_specs=in_specs, out_specs=[out_spec, out_spec],
        out_shape=[jax.ShapeDtypeStruct((m_all, D), F32), jax.ShapeDtypeStruct((m_all, D), BF16)],
        input_output_aliases={2: 0, 3: 1} if has_acc else {},
        compiler_params=_cp(("parallel",), 32), name="entry_norm")(*args)


def _matmul_kernel(x_ref, w_ref, o_ref, wb_ref, *, transposed):
    @pl.when(pl.program_id(1) == 0)
    def _():
        wb_ref[...] = w_ref[...].astype(BF16)

    if transposed:
        y = lax.dot_general(x_ref[...], wb_ref[...], (((1,), (1,)), ((), ())), preferred_element_type=F32)
    else:
        y = _dot(x_ref[...], wb_ref[...])
    o_ref[...] = y.astype(o_ref.dtype)


def matmul(x, w, layer, n_cols, tn, transposed=False):
    M, K = x.shape
    tm = _row_tile(M, MM_ROWS_CAP)
    tn = _tile(n_cols, tn)
    if transposed:
        w_spec = pl.BlockSpec((None, tn, K), lambda j, i: (layer, j, 0))
        w_tile = (tn, K)
    else:
        w_spec = pl.BlockSpec((None, K, tn), lambda j, i: (layer, 0, j))
        w_tile = (K, tn)
    return pl.pallas_call(
        functools.partial(_matmul_kernel, transposed=transposed), grid=(n_cols // tn, M // tm),
        in_specs=[pl.BlockSpec((tm, K), lambda j, i: (i, 0)), w_spec],
        out_specs=pl.BlockSpec((tm, tn), lambda j, i: (i, j)),
        out_shape=jax.ShapeDtypeStruct((M, n_cols), F32),
        scratch_shapes=[pltpu.VMEM(w_tile, BF16)],
        compiler_params=_cp(("parallel", "arbitrary"), 48), name="in_proj")(x, w)


def _outproj_kernel(*refs, n_act):
    acts = refs[:n_act]
    w_ref, x_ref, g_ref, xo_ref, xn_ref = refs[n_act:]
    a = jnp.concatenate([a_ref[...].astype(BF16) for a_ref in acts], axis=1)
    x_new = x_ref[...] + _dot(a, w_ref[...])
    xo_ref[...] = x_new
    xn_ref[...] = _rms(x_new, g_ref[...]).astype(xn_ref.dtype)


def out_proj(acts, w, layer, x, g):
    M, D = x.shape
    K = w.shape[1]
    tm = _row_tile(M, OUT_ROWS_CAP)
    row = lambda i: (i, 0)
    fixed = lambda i: (0, 0)
    in_specs = [pl.BlockSpec((tm, a.shape[1]), row) for a in acts]
    in_specs += [pl.BlockSpec((None, K, D), lambda i: (layer, 0, 0), pipeline_mode=pl.Buffered(1)),
                 pl.BlockSpec((tm, D), row), pl.BlockSpec((1, D), fixed)]
    return pl.pallas_call(
        functools.partial(_outproj_kernel, n_act=len(acts)), grid=(M // tm,),
        in_specs=in_specs,
        out_specs=[pl.BlockSpec((tm, D), row), pl.BlockSpec((tm, D), row)],
        out_shape=[jax.ShapeDtypeStruct((M, D), F32), jax.ShapeDtypeStruct((M, D), BF16)],
        compiler_params=_cp(("parallel",), 52), name="out_proj")(*acts, w, x, g.reshape(1, D))


def _mlp_kernel(xn_ref, x_ref, wu_ref, wd_ref, g_ref, *rest, emit_x, emit_w, n_acc):
    rest = list(rest[n_acc:])
    xo_ref = rest.pop(0) if emit_x else rest.pop()
    xn_out_ref = rest.pop(0)
    f = pl.program_id(1)

    @pl.when(f == 0)
    def _():
        xo_ref[...] = x_ref[...]

    wu = wu_ref[...].astype(BF16)
    wd = wd_ref[...].astype(BF16)
    if emit_w:
        wu_o, wd_o = rest
        wu_o[...] = wu
        wd_o[...] = wd
    h = _dot(xn_ref[...], wu)
    h = jnp.square(jnp.maximum(h, 0.0)).astype(BF16)
    xo_ref[...] += _dot(h, wd)

    @pl.when(f == pl.num_programs(1) - 1)
    def _():
        xn_out_ref[...] = _rms(xo_ref[...], g_ref[...]).astype(xn_out_ref.dtype)


def mlp(xn, x, w_up, w_down, layer, g, out_dtype, row0, nrows, *, tf, emit_x=True, emit_w=False,
        acc=None, out_rows=None):
    D = x.shape[1]
    F = w_up.shape[-1]
    tm = _row_tile(nrows, MM_ROWS_CAP)
    assert row0 % tm == 0 and F % tf == 0
    assert not emit_w or nrows == tm
    off = row0 // tm
    out_off = off if out_rows is not None else 0
    m_out = out_rows if out_rows is not None else nrows
    row_in = lambda i, f: (i + off, 0)
    once = pl.Buffered(1)
    row_out = pl.BlockSpec((tm, D), lambda i, f: (i + out_off, 0), pipeline_mode=once)
    if layer is None:
        wu_spec = pl.BlockSpec((D, tf), lambda i, f: (0, f))
        wd_spec = pl.BlockSpec((tf, D), lambda i, f: (f, 0))
    else:
        wu_spec = pl.BlockSpec((None, D, tf), lambda i, f: (layer, 0, f))
        wd_spec = pl.BlockSpec((None, tf, D), lambda i, f: (layer, f, 0))
    in_specs = [pl.BlockSpec((tm, D), row_in), pl.BlockSpec((tm, D), row_in, pipeline_mode=once),
                wu_spec, wd_spec, pl.BlockSpec((1, D), lambda i, f: (0, 0))]
    args = [xn, x, w_up, w_down, g.reshape(1, D)]
    out_specs, out_shape = [], []
    if emit_x:
        out_specs.append(row_out)
        out_shape.append(jax.ShapeDtypeStruct((m_out, D), F32))
    out_specs.append(row_out)
    out_shape.append(jax.ShapeDtypeStruct((m_out, D), out_dtype))
    aliases = {}
    if acc is not None:
        assert emit_x and out_rows is not None
        in_specs += [pl.BlockSpec(memory_space=pl.ANY)] * 2
        args += list(acc)
        aliases = {5: 0, 6: 1}
    if emit_w:
        out_specs += [pl.BlockSpec((D, tf), lambda i, f: (0, f)), pl.BlockSpec((tf, D), lambda i, f: (f, 0))]
        out_shape += [jax.ShapeDtypeStruct((D, F), BF16), jax.ShapeDtypeStruct((F, D), BF16)]
    outs = pl.pallas_call(
        functools.partial(_mlp_kernel, emit_x=emit_x, emit_w=emit_w, n_acc=0 if acc is None else 2),
        grid=(nrows // tm, F // tf), in_specs=in_specs, out_specs=out_specs, out_shape=out_shape,
        scratch_shapes=[] if emit_x else [pltpu.VMEM((tm, D), F32)],
        input_output_aliases=aliases,
        compiler_params=_cp(("parallel", "arbitrary"), MLP_VMEM_MIB), name="mlp")(*args)
    return list(outs)


def mlp_all_rows(xn, x, w_up, w_down, layer, g, out_dtype):
    m_all = x.shape[0]
    tm = _row_tile(m_all, MM_ROWS_CAP)
    if tm == m_all:
        return mlp(xn, x, w_up, w_down, layer, g, out_dtype, 0, m_all, tf=MLP_TF_F32)
    x0, xn0, wu_b, wd_b = mlp(xn, x, w_up, w_down, layer, g, out_dtype, 0, tm, tf=MLP_TF_F32,
                              emit_w=True, out_rows=m_all)
    return mlp(xn, x, wu_b, wd_b, None, g, out_dtype, tm, m_all - tm, tf=MLP_TF_BF16,
               acc=(x0, xn0), out_rows=m_all)


def _rglru_gates(xc, wg_ref, ba, bx, lam):
    C = xc.shape[1]
    xcb = xc.astype(BF16)
    r_parts, i_parts = [], []
    for c in range(C // GATE_BLOCK):
        lo, hi = c * GATE_BLOCK, (c + 1) * GATE_BLOCK
        gates = _dot(xcb[:, lo:hi], wg_ref[c])
        r_parts.append(gates[:, :GATE_BLOCK])
        i_parts.append(gates[:, GATE_BLOCK:])
    r = _sigmoid(jnp.concatenate(r_parts, axis=1) + ba)
    ig = _sigmoid(jnp.concatenate(i_parts, axis=1) + bx)
    log_a = (-LRU_C * _softplus(-lam)) * r
    a = jnp.exp(log_a)
    u = jnp.sqrt(-jnp.tanh(log_a) * (a * a + 1.0)) * (ig * xc)
    return a, u


def _rglru_prompt_kernel(xr_ref, gr_ref, cw_ref, cb_ref, wg_ref, ba_ref, bx_ref, lam_ref,
                         y_ref, tail_ref, hlast_ref, xp_ref, a_ref, u_ref, h_ref):
    t = pl.program_id(1)
    tT, C = xr_ref.shape
    nk = cw_ref.shape[0]

    @pl.when(t == 0)
    def _():
        xp_ref[0:SUBLANES, :] = jnp.zeros((SUBLANES, C), F32)
        h_ref[...] = jnp.zeros_like(h_ref)

    x = xr_ref[...]
    xp_ref[SUBLANES:SUBLANES + tT, :] = x
    xc = cb_ref[...] + cw_ref[nk - 1:nk, :] * x
    for k in range(nk - 1):
        off = SUBLANES - (nk - 1) + k
        xc = xc + cw_ref[k:k + 1, :] * xp_ref[off:off + tT, :]
    xp_ref[0:SUBLANES, :] = x[tT - SUBLANES:tT, :]

    a, u = _rglru_gates(xc, wg_ref, ba_ref[...], bx_ref[...], lam_ref[...])
    a_ref[...] = a
    u_ref[...] = u

    rowi = lax.broadcasted_iota(jnp.int32, (SUBLANES, C), 0)

    def body(g, h):
        r0 = pl.multiple_of(g * SUBLANES, SUBLANES)
        a8 = a_ref[pl.ds(r0, SUBLANES), :]
        u8 = u_ref[pl.ds(r0, SUBLANES), :]
        for s in (1, 2, 4):
            a_sh = pltpu.roll(a8, s, 0)
            u_sh = pltpu.roll(u8, s, 0)
            m = rowi >= s
            u8 = jnp.where(m, a8 * u_sh + u8, u8)
            a8 = jnp.where(m, a8 * a_sh, a8)
        h8 = a8 * h + u8
        u_ref[pl.ds(r0, SUBLANES), :] = h8
        return h8[SUBLANES - 1:SUBLANES, :]

    h = lax.fori_loop(0, tT // SUBLANES, body, h_ref[...])
    h_ref[...] = h
    y_ref[...] = (u_ref[...] * _gelu_tanh(gr_ref[...])).astype(y_ref.dtype)

    @pl.when(t == pl.num_programs(1) - 1)
    def _():
        tail_ref[0] = x[tT - SUBLANES:tT, :]
        hlast_ref[0] = h


def rglru_prompt(proj, B, T, cw, cb, wg, ba, bx, lam):
    C = cw.shape[1]
    tT = _tile(T, 256)
    nT = T // tT
    fixed2 = lambda b, t: (0, 0)
    vec = pl.BlockSpec((1, C), fixed2)
    return pl.pallas_call(
        _rglru_prompt_kernel, grid=(B, nT),
        in_specs=[pl.BlockSpec((tT, C), lambda b, t: (b * nT + t, 0)),
                  pl.BlockSpec((tT, C), lambda b, t: (b * nT + t, 1)),
                  pl.BlockSpec(cw.shape, fixed2), vec,
                  pl.BlockSpec(wg.shape, lambda b, t: (0, 0, 0)), vec, vec, vec],
        out_specs=[pl.BlockSpec((tT, C), lambda b, t: (b * nT + t, 0)),
                   pl.BlockSpec((1, SUBLANES, C), lambda b, t: (b, 0, 0)),
                   pl.BlockSpec((1, 1, C), lambda b, t: (b, 0, 0))],
        out_shape=[jax.ShapeDtypeStruct((proj.shape[0], C), BF16),
                   jax.ShapeDtypeStruct((B, SUBLANES, C), F32),
                   jax.ShapeDtypeStruct((B, 1, C), F32)],
        scratch_shapes=[pltpu.VMEM((SUBLANES + tT, C), F32), pltpu.VMEM((tT, C), F32),
                        pltpu.VMEM((tT, C), F32), pltpu.VMEM((1, C), F32)],
        compiler_params=_cp(("parallel", "arbitrary"), 40), name="rglru_prompt",
    )(proj, proj, cw, cb.reshape(1, C), wg, ba.reshape(1, C), bx.reshape(1, C), lam.reshape(1, C))


CONF_STRIDE = 4
LANES = 128


def _conf_prompt_kernel(ac_ref, bc_ref, dw_ref, db_ref, lg_ref, lb_ref,
                        y_ref, tail_ref, gp_ref, yc_ref):
    t = pl.program_id(1)
    tT, C = ac_ref.shape
    nk = dw_ref.shape[0]
    pad = gp_ref.shape[1] - tT
    nslab = C // LANES
    span = CONF_STRIDE * SUBLANES

    @pl.when(t == 0)
    def _():
        gp_ref[:, 0:pad, :] = jnp.zeros((nslab, pad, LANES), F32)

    glu = ac_ref[...] * _sigmoid(bc_ref[...])
    for c in range(nslab):
        gp_ref[c, pad:pad + tT, :] = glu[:, c * LANES:(c + 1) * LANES]
    for c in range(nslab):
        lo = c * LANES
        for r0 in range(0, tT, span):
            accs = [jnp.broadcast_to(db_ref[:, lo:lo + LANES], (SUBLANES, LANES)) for _ in range(CONF_STRIDE)]
            for k in range(nk):
                w = dw_ref[k:k + 1, lo:lo + LANES]
                off = pad - (nk - 1) + k + r0
                for p in range(CONF_STRIDE):
                    accs[p] = accs[p] + gp_ref[c, pl.ds(off + p, SUBLANES, stride=CONF_STRIDE), :] * w
            for p in range(CONF_STRIDE):
                yc_ref[c, pl.ds(r0 + p, SUBLANES, stride=CONF_STRIDE), :] = accs[p]
    yc = jnp.concatenate([yc_ref[c] for c in range(nslab)], axis=1)
    yn = _layernorm(yc, lg_ref[...], lb_ref[...])
    y_ref[...] = (yn * _sigmoid(yn)).astype(y_ref.dtype)

    @pl.when(t == pl.num_programs(1) - 1)
    def _():
        tail_ref[0] = jnp.concatenate([gp_ref[c, tT:tT + pad, :] for c in range(nslab)], axis=1)

    gp_ref[:, 0:pad, :] = gp_ref[:, tT:tT + pad, :]


def conf_prompt(proj, B, T, dw, db, lg, lb):
    nk, C = dw.shape
    tT = _tile(T, 128)
    nT = T // tT
    pad = -(-(nk - 1) // SUBLANES) * SUBLANES
    assert tT % (CONF_STRIDE * SUBLANES) == 0 and C % LANES == 0
    fixed2 = lambda b, t: (0, 0)
    vec = pl.BlockSpec((1, C), fixed2)
    return pl.pallas_call(
        _conf_prompt_kernel, grid=(B, nT),
        in_specs=[pl.BlockSpec((tT, C), lambda b, t: (b * nT + t, 2)),
                  pl.BlockSpec((tT, C), lambda b, t: (b * nT + t, 3)),
                  pl.BlockSpec((nk, C), fixed2), vec, vec, vec],
        out_specs=[pl.BlockSpec((tT, C), lambda b, t: (b * nT + t, 0)),
                   pl.BlockSpec((1, pad, C), lambda b, t: (b, 0, 0))],
        out_shape=[jax.ShapeDtypeStruct((proj.shape[0], C), BF16), jax.ShapeDtypeStruct((B, pad, C), F32)],
        scratch_shapes=[pltpu.VMEM((C // LANES, pad + tT, LANES), F32),
                        pltpu.VMEM((C // LANES, tT, LANES), F32)],
        compiler_params=_cp(("parallel", "arbitrary"), 32), name="conf_prompt",
    )(proj, proj, dw, db.reshape(1, C), lg.reshape(1, C), lb.reshape(1, C))


def _ab_sample_kernel(*refs, has_acc):
    (proj_ref, rconv_ref, h0_ref, cconv_ref,
     cw_ref, cb_ref, wg_ref, ba_ref, bx_ref, lam_ref, dw_ref, db_ref, lg_ref, lb_ref) = refs[:14]
    yr_ref, yc_ref, rconv_o, h_o, cconv_o = refs[14 + int(has_acc):]
    C = h0_ref.shape[1]
    nk = cw_ref.shape[0]
    nd = dw_ref.shape[0]
    x_r = proj_ref[:, 0:C]
    g_r = proj_ref[:, C:2 * C]
    glu = proj_ref[:, 2 * C:3 * C] * _sigmoid(proj_ref[:, 3 * C:4 * C])

    xc = cb_ref[...] + cw_ref[nk - 1:nk, :] * x_r
    for k in range(nk - 1):
        xc = xc + cw_ref[k:k + 1, :] * rconv_ref[k]
    for k in range(nk - 2):
        rconv_o[k] = rconv_ref[k + 1]
    rconv_o[nk - 2] = x_r

    yc = db_ref[...] + dw_ref[nd - 1:nd, :] * glu
    for k in range(nd - 1):
        yc = yc + dw_ref[k:k + 1, :] * cconv_ref[k]
    for k in range(nd - 2):
        cconv_o[k] = cconv_ref[k + 1]
    cconv_o[nd - 2] = glu

    a, u = _rglru_gates(xc, wg_ref, ba_ref[...], bx_ref[...], lam_ref[...])
    h = a * h0_ref[...] + u
    h_o[...] = h
    yr_ref[...] = h * _gelu_tanh(g_r)
    yn = _layernorm(yc, lg_ref[...], lb_ref[...])
    yc_ref[...] = yn * _sigmoid(yn)


def ab_sample(proj, row0, rconv_t, h0, cconv_t, layer, cconv_acc, cw, cb, wg, ba, bx, lam, dw, db, lg, lb):
    _, B, C = h0.shape
    bt = _tile(B, 32)
    row = lambda i: (i, 0)
    fixed = lambda i: (0, 0)
    vec = pl.BlockSpec((1, C), fixed)
    nr, nc = rconv_t.shape[1], cconv_t.shape[1]
    st_c = pl.BlockSpec((None, nc, bt, C), lambda i: (layer, 0, i, 0))
    act = pl.BlockSpec((bt, C), row)
    has_acc = cconv_acc is not None
    assert row0 % bt == 0
    off = row0 // bt
    in_specs = [pl.BlockSpec((bt, proj.shape[1]), lambda i: (i + off, 0)),
                pl.BlockSpec((None, nr, bt, C), lambda i: (layer, 0, i, 0)),
                pl.BlockSpec((None, bt, C), lambda i: (layer, i, 0)), st_c,
                pl.BlockSpec(cw.shape, fixed), vec, pl.BlockSpec(wg.shape, lambda i: (0, 0, 0)),
                vec, vec, vec, pl.BlockSpec(dw.shape, fixed), vec, vec, vec]
    args = [proj, rconv_t, h0, cconv_t, cw, cb.reshape(1, C), wg, ba.reshape(1, C), bx.reshape(1, C),
            lam.reshape(1, C), dw, db.reshape(1, C), lg.reshape(1, C), lb.reshape(1, C)]
    if has_acc:
        in_specs.append(pl.BlockSpec(memory_space=pl.ANY))
        args.append(cconv_acc)
    return pl.pallas_call(
        functools.partial(_ab_sample_kernel, has_acc=has_acc), grid=(B // bt,),
        in_specs=in_specs,
        out_specs=[act, act, pl.BlockSpec((nr, bt, C), lambda i: (0, i, 0)), act, st_c],
        out_shape=[jax.ShapeDtypeStruct((B, C), F32), jax.ShapeDtypeStruct((B, C), F32),
                   jax.ShapeDtypeStruct((nr, B, C), F32), jax.ShapeDtypeStruct((B, C), F32),
                   jax.ShapeDtypeStruct(cconv_t.shape, F32)],
        input_output_aliases={len(args) - 1: 4} if has_acc else {},
        compiler_params=_cp(("parallel",), 40), name="ab_sample",
    )(*args)


def _mlstm_prompt_kernel(q_ref, k_ref, v_ref, o_ref, g_ref, bif_ref, hng_ref,
                         y_ref, C_ref, n_ref, m_ref, *, heads):
    c = pl.program_id(1)
    L = q_ref.shape[0]
    H = heads
    DK = q_ref.shape[1] // H
    DV = v_ref.shape[1] // H
    scale = DK ** -0.5

    @pl.when(c == 0)
    def _():
        C_ref[...] = jnp.zeros_like(C_ref)
        n_ref[...] = jnp.zeros_like(n_ref)
        m_ref[...] = jnp.zeros_like(m_ref)

    ti = lax.broadcasted_iota(jnp.int32, (L, L), 0)
    si = lax.broadcasted_iota(jnp.int32, (L, L), 1)
    causal = si <= ti
    tri_lo = causal.astype(BF16)
    tri_up = (ti <= si).astype(BF16)

    gates = g_ref[...] + bif_ref[...]
    lf = _log_sigmoid(gates)
    gates_t = gates.T
    lf_t = lf.T
    hi, mid, lo = _split3(lf)
    b_cols = (_dot(tri_lo, lo) + _dot(tri_lo, mid)) + _dot(tri_lo, hi)
    hi, mid, lo = _split3(lf_t)
    b_rows = (_dot(lo, tri_up) + _dot(mid, tri_up)) + _dot(hi, tri_up)

    m_all = m_ref[0]
    n_all = n_ref[0]
    hs = range(H)
    dk = lambda h: slice(h * DK, (h + 1) * DK)
    dv = lambda h: slice(h * DV, (h + 1) * DV)
    ig_col = [gates[:, h:h + 1] for h in hs]
    ig_row = [gates_t[h:h + 1, :] for h in hs]
    b_col = [b_cols[:, H + h:H + h + 1] for h in hs]
    b_row = [b_rows[H + h:H + h + 1, :] for h in hs]
    m_prev = [m_all[h:h + 1, 0:1] for h in hs]
    n_row = [n_all[h:h + 1, :] for h in hs]

    qs = [q_ref[:, dk(h)] * scale for h in hs]
    qb = [qs[h].astype(BF16) for h in hs]
    kb = [k_ref[:, dk(h)].astype(BF16) for h in hs]
    vb = [v_ref[:, dv(h)].astype(BF16) for h in hs]
    dmat = [jnp.where(causal, b_col[h] - b_row[h] + ig_row[h], -jnp.inf) for h in hs]
    inter = [b_col[h] + m_prev[h] for h in hs]
    m_t = [jnp.maximum(inter[h], jnp.max(dmat[h], axis=-1, keepdims=True)) for h in hs]
    qk = [lax.dot_general(qb[h], kb[h], (((1,), (1,)), ((), ())), preferred_element_type=F32) for h in hs]
    s = [qk[h] * jnp.exp(dmat[h] - m_t[h]) for h in hs]
    scale_in = [jnp.exp(inter[h] - m_t[h]) for h in hs]
    qC = [_dot(qb[h], C_ref[0, h].astype(BF16)) for h in hs]
    num = [_dot(s[h].astype(BF16), vb[h]) + scale_in[h] * qC[h] for h in hs]
    den = [jnp.sum(s[h], axis=-1, keepdims=True)
           + scale_in[h] * jnp.sum(qs[h] * n_row[h], axis=-1, keepdims=True) for h in hs]
    hh = [num[h] / jnp.maximum(jnp.abs(den[h]), jnp.exp(-m_t[h])) for h in hs]
    for h in hs:
        hn = hh[h] * lax.rsqrt(jnp.mean(hh[h] * hh[h], axis=-1, keepdims=True) + EPS)
        y = (hn * hng_ref[:, dv(h)]) * _sigmoid(o_ref[:, dv(h)])
        y_ref[:, dv(h)] = y.astype(y_ref.dtype)
    b_last = [b_col[h][L - 1:L, :] for h in hs]
    g_row = [b_last[h] - b_row[h] + ig_row[h] for h in hs]
    g_col = [b_last[h] - b_col[h] + ig_col[h] for h in hs]
    m_new = [jnp.maximum(b_last[h] + m_prev[h], jnp.max(g_row[h], axis=-1, keepdims=True)) for h in hs]
    decay = [jnp.exp(b_last[h] + m_prev[h] - m_new[h]) for h in hs]
    kw = [k_ref[:, dk(h)] * jnp.exp(g_col[h] - m_new[h]) for h in hs]
    for h in hs:
        C_ref[0, h] = decay[h] * C_ref[0, h] + lax.dot_general(
            kw[h].astype(BF16), vb[h], (((0,), (0,)), ((), ())), preferred_element_type=F32)
    n_ref[0] = jnp.concatenate([decay[h] * n_row[h] + jnp.sum(kw[h], axis=0, keepdims=True) for h in hs], axis=0)
    m_ref[0] = jnp.concatenate([jnp.broadcast_to(m_new[h], (1, m_ref.shape[2])) for h in hs], axis=0)


def mlstm_prompt(proj, gates, bif_row, hng, B, T, heads):
    QK = proj.shape[1] // 6
    V = 2 * QK
    H = heads
    DK, DV = QK // H, V // H
    L = ML_CHUNK if T % ML_CHUNK == 0 else T
    nc = T // L
    rowmap = lambda j: (lambda b, c: (b * nc + c, j))
    fixed2 = lambda b, c: (0, 0)
    return pl.pallas_call(
        functools.partial(_mlstm_prompt_kernel, heads=H), grid=(B, nc),
        in_specs=[pl.BlockSpec((L, QK), rowmap(0)), pl.BlockSpec((L, QK), rowmap(1)),
                  pl.BlockSpec((L, V), rowmap(1)), pl.BlockSpec((L, V), rowmap(2)),
                  pl.BlockSpec((L, LANES), rowmap(0)), pl.BlockSpec((1, LANES), fixed2),
                  pl.BlockSpec((1, V), fixed2)],
        out_specs=[pl.BlockSpec((L, V), rowmap(0)),
                   pl.BlockSpec((1, H, DK, DV), lambda b, c: (b, 0, 0, 0)),
                   pl.BlockSpec((1, H, DK), lambda b, c: (b, 0, 0)),
                   pl.BlockSpec((1, H, LANES), lambda b, c: (b, 0, 0))],
        out_shape=[jax.ShapeDtypeStruct((proj.shape[0], V), BF16),
                   jax.ShapeDtypeStruct((B, H, DK, DV), F32),
                   jax.ShapeDtypeStruct((B, H, DK), F32),
                   jax.ShapeDtypeStruct((B, H, LANES), F32)],
        compiler_params=_cp(("parallel", "arbitrary"), 40), name="mlstm_prompt",
    )(proj, proj, proj, proj, gates, bif_row, hng.reshape(1, V))


def _mlstm_gate_terms(ig, fg, m):
    lf = _log_sigmoid(fg)
    inter = lf + m
    m_t = jnp.maximum(inter, ig)
    return m_t, jnp.exp(ig - m_t), jnp.exp(inter - m_t)


def _mlstm_sample_kernel(*refs, has_acc):
    (qt_ref, kt_ref, qr_ref, kr_ref, v_ref, o_ref, gr_ref, gc_ref, bifr_ref, bifc_ref,
     hng_ref, C_ref, n_ref) = refs[:13]
    y_ref, Co_ref, no_ref, mo_ref = refs[13 + int(has_acc):]
    bb, H, DK = qr_ref.shape
    scale = DK ** -0.5
    bifr = bifr_ref[...]
    bifc = bifc_ref[...]
    for j in range(bb):
        gr = gr_ref[j]
        gc = gc_ref[j]
        mt_r, es_r, sc_r = _mlstm_gate_terms(gr[:, 0:H] + bifr[:, 0:H], gr[:, H:2 * H] + bifr[:, H:2 * H],
                                             gr[:, 2 * H:3 * H])
        mt_c, es_c, sc_c = _mlstm_gate_terms(gc[0:H, :] + bifc[0:H, :], gc[H:2 * H, :] + bifc[H:2 * H, :],
                                             gc[2 * H:3 * H, :])
        qr = qr_ref[j] * scale
        kr = kr_ref[j]
        n = n_ref[j]
        v = v_ref[j]
        qk = jnp.sum(qr * kr, axis=1, keepdims=True)
        qn = jnp.sum(qr * n, axis=1, keepdims=True)
        qt = qt_ref[j] * scale
        kts = kt_ref[j] * es_r
        rows = []
        for h in range(H):
            Ch = C_ref[j, h]
            rows.append(jnp.sum(Ch * qt[:, h:h + 1], axis=0, keepdims=True))
            Co_ref[j, h] = sc_r[:, h:h + 1] * Ch + kts[:, h:h + 1] * v[h:h + 1, :]
        qC = jnp.concatenate(rows, axis=0)
        s = qk * es_c
        num = s * v + sc_c * qC
        den = s + sc_c * qn
        hh = num / jnp.maximum(jnp.abs(den), jnp.exp(-mt_c))
        hn = hh * lax.rsqrt(jnp.mean(hh * hh, axis=-1, keepdims=True) + EPS)
        y_ref[j] = (hn * hng_ref[...]) * _sigmoid(o_ref[j])
        no_ref[j] = sc_c * n + es_c * kr
        mo_ref[j] = mt_r


def mlstm_sample(q, k, v, o, gates, m, bif, hng, C, n, layer, C_acc):
    B, H, DK = q.shape
    DV = v.shape[2]
    bb = _tile(B, 8)
    gm = jnp.concatenate([gates, m], axis=1)
    qt = jnp.swapaxes(q, 1, 2)
    kt = jnp.swapaxes(k, 1, 2)
    b3 = lambda i: (i, 0, 0)
    fixed = lambda i: (0, 0)
    blk = lambda a: pl.BlockSpec((bb,) + a.shape[1:], b3)
    C_spec = pl.BlockSpec((None, bb, H, DK, DV), lambda i: (layer, i, 0, 0, 0))
    has_acc = C_acc is not None
    in_specs = [blk(qt), blk(kt), blk(q), blk(k), blk(v), blk(o),
                pl.BlockSpec((bb, 1, 3 * H), b3), pl.BlockSpec((bb, 3 * H, 1), b3),
                pl.BlockSpec((1, 2 * H), fixed), pl.BlockSpec((2 * H, 1), fixed),
                pl.BlockSpec((H, DV), fixed), C_spec,
                pl.BlockSpec((None, bb, H, DK), lambda i: (layer, i, 0, 0))]
    args = [qt, kt, q, k, v, o, gm.reshape(B, 1, 3 * H), gm.reshape(B, 3 * H, 1),
            bif.reshape(1, 2 * H), bif.reshape(2 * H, 1), hng.reshape(H, DV), C, n]
    if has_acc:
        in_specs.append(pl.BlockSpec(memory_space=pl.ANY))
        args.append(C_acc)
    return pl.pallas_call(
        functools.partial(_mlstm_sample_kernel, has_acc=has_acc), grid=(B // bb,),
        in_specs=in_specs,
        out_specs=[blk(v), C_spec, pl.BlockSpec((bb, H, DK), b3), pl.BlockSpec((bb, 1, H), b3)],
        out_shape=[jax.ShapeDtypeStruct((B, H, DV), F32), jax.ShapeDtypeStruct(C.shape, F32),
                   jax.ShapeDtypeStruct((B, H, DK), F32), jax.ShapeDtypeStruct((B, 1, H), F32)],
        input_output_aliases={len(args) - 1: 1} if has_acc else {},
        compiler_params=_cp(("parallel",), 48), name="mlstm_sample",
    )(*args)


def _blockdiag(w, per):
    H, d, _ = w.shape
    w4 = w.reshape(H // per, per, d, d)
    eye = jnp.eye(per, dtype=w.dtype)
    return (w4[:, :, :, None, :] * eye[None, :, None, :, None]).reshape(H // per, per * d, per * d)


def _prep_weights(P):
    W = {}
    W['ab_w_out'] = P['ab_w_out'].astype(BF16)
    W['ml_w_out'] = P['ml_w_out'].astype(BF16)
    heads = P['ml_b_if'].shape[1] // 2
    n_main = P['ml_w_in'].shape[2] - 2 * heads
    W['ml_w_in_t'] = jnp.swapaxes(P['ml_w_in'], 1, 2)
    W['ml_w_gate'] = jnp.pad(P['ml_w_in'][:, :, n_main:], ((0, 0), (0, 0), (0, LANES - 2 * heads)))
    W['ml_bif_row'] = jnp.pad(P['ml_b_if'], ((0, 0), (0, LANES - 2 * heads)))[:, None, :]
    d = P['rglru_wa'].shape[2]
    per = GATE_BLOCK // d
    wa = jax.vmap(lambda w: _blockdiag(w, per))(P['rglru_wa'])
    wx = jax.vmap(lambda w: _blockdiag(w, per))(P['rglru_wx'])
    W['rglru_wg'] = jnp.concatenate([wa, wx], axis=-1).astype(BF16)
    return W


def _set_rows(y_all, y_rows, row0):
    return lax.dynamic_update_slice(y_all, y_rows.astype(y_all.dtype), (row0, 0))


def _run_trunk(xp, xs, P, W, Bp, Tp, states):
    depth = P['mlp_norm'].shape[0]
    heads = P['ml_b_if'].shape[1] // 2
    n_main = P['ml_w_in'].shape[2] - 2 * heads
    Mp, Bs = xp.shape[0], xs.shape[0]
    m_all = Mp + Bs
    p_rconv, p_rh, p_cconv, p_C, p_n, p_m = [], [], [], [], [], []
    s_rconv, s_rh, s_n, s_m = [], [], [], []
    cconv_acc = None
    C_acc = None
    x, xn = entry_norm(xs, P['ab_norm'][0], m_all, Mp, entry_norm(xp, P['ab_norm'][0], m_all, 0, None))
    for l in range(depth):
        i = l // 2
        if l % 2 == 0:
            proj = matmul(xn, P['ab_w_in'], i, P['ab_w_in'].shape[2], 1024)
            rg = (P['rglru_conv_w'][i], P['rglru_conv_b'][i], W['rglru_wg'][i], P['rglru_ba'][i],
                  P['rglru_bx'][i], P['rglru_lambda'][i])
            cf = (P['conf_dw_w'][i], P['conf_dw_b'][i], P['conf_ln_g'][i], P['conf_ln_b'][i])
            y_r, tail_r, h_last = rglru_prompt(proj, Bp, Tp, *rg)
            y_c, tail_c = conf_prompt(proj, Bp, Tp, *cf)
            nr = P['rglru_conv_w'].shape[1] - 1
            ncf = P['conf_dw_w'].shape[1] - 1
            p_rconv.append(tail_r[:, tail_r.shape[1] - nr:])
            p_rh.append(h_last[:, 0])
            p_cconv.append(tail_c[:, tail_c.shape[1] - ncf:])
            ys_r, ys_c, rconv_new, h_new, cconv_acc = ab_sample(proj, Mp, states[0], states[1], states[2], i,
                                                                cconv_acc, *rg, *cf)
            s_rconv.append(rconv_new); s_rh.append(h_new)
            acts = [_set_rows(y_r, ys_r, Mp), _set_rows(y_c, ys_c, Mp)]
            x, xn = out_proj(acts, W['ab_w_out'], i, x, P['mlp_norm'][l])
        else:
            proj = matmul(xn, W['ml_w_in_t'], i, n_main, 1024, transposed=True)
            gates = matmul(xn, W['ml_w_gate'], i, LANES, LANES)
            hng = P['ml_hnorm_g'][i]
            y, C, n, m = mlstm_prompt(proj, gates, W['ml_bif_row'][i], hng, Bp, Tp, heads)
            p_C.append(C); p_n.append(n); p_m.append(m[:, :, 0])
            QK = n_main // 6
            V = 2 * QK
            ps = proj[Mp:]
            q = ps[:, :QK].reshape(Bs, heads, QK // heads)
            k = ps[:, QK:2 * QK].reshape(Bs, heads, QK // heads)
            v = ps[:, 2 * QK:2 * QK + V].reshape(Bs, heads, V // heads)
            o = ps[:, 2 * QK + V:].reshape(Bs, heads, V // heads)
            ys, C_acc, n, m = mlstm_sample(q, k, v, o, gates[Mp:, :2 * heads], states[5][i], P['ml_b_if'][i],
                                           hng, states[3], states[4], i, C_acc)
            s_n.append(n); s_m.append(m[:, 0])
            x, xn = out_proj([_set_rows(y, ys.reshape(Bs, V), Mp)], W['ml_w_out'], i, x, P['mlp_norm'][l])
        if l < depth - 1:
            g_next = P['ab_norm'][(l + 1) // 2] if (l + 1) % 2 == 0 else P['ml_norm'][(l + 1) // 2]
            x, xn = mlp_all_rows(xn, x, P['mlp_w_up'], P['mlp_w_down'], l, g_next, BF16)
    last = depth - 1
    ys, wu_b, wd_b = mlp(xn, x, P['mlp_w_up'], P['mlp_w_down'], last, P['final_norm'], F32, Mp, Bs,
                         tf=MLP_TF_F32, emit_x=False, emit_w=True)
    (yp,) = mlp(xn, x, wu_b, wd_b, None, P['final_norm'], F32, 0, Mp, tf=MLP_TF_BF16, emit_x=False)
    prompt_out = (jnp.stack(p_rconv), jnp.stack(p_rh), jnp.stack(p_cconv),
                  jnp.stack(p_C), jnp.stack(p_n), jnp.stack(p_m))
    sample_out = (jnp.swapaxes(jnp.stack(s_rconv), 1, 2), jnp.stack(s_rh), jnp.swapaxes(cconv_acc, 1, 2),
                  C_acc, jnp.stack(s_n), jnp.stack(s_m))
    return yp, ys, prompt_out, sample_out


def kernel(x_prompt, x_sample, state_rglru_conv, state_rglru_h, state_conf_conv, state_mlstm_C, state_mlstm_n, state_mlstm_m, ab_norm, ab_w_in, rglru_conv_w, rglru_conv_b, rglru_wa, rglru_ba, rglru_wx, rglru_bx, rglru_lambda, conf_dw_w, conf_dw_b, conf_ln_g, conf_ln_b, ab_w_out, ml_norm, ml_w_in, ml_b_if, ml_hnorm_g, ml_w_out, mlp_norm, mlp_w_up, mlp_w_down, final_norm):
    P = {'ab_norm': ab_norm, 'ab_w_in': ab_w_in, 'rglru_conv_w': rglru_conv_w, 'rglru_conv_b': rglru_conv_b,
         'rglru_wa': rglru_wa, 'rglru_ba': rglru_ba, 'rglru_wx': rglru_wx, 'rglru_bx': rglru_bx,
         'rglru_lambda': rglru_lambda, 'conf_dw_w': conf_dw_w, 'conf_dw_b': conf_dw_b,
         'conf_ln_g': conf_ln_g, 'conf_ln_b': conf_ln_b, 'ab_w_out': ab_w_out, 'ml_norm': ml_norm,
         'ml_w_in': ml_w_in, 'ml_b_if': ml_b_if, 'ml_hnorm_g': ml_hnorm_g, 'ml_w_out': ml_w_out,
         'mlp_norm': mlp_norm, 'mlp_w_up': mlp_w_up, 'mlp_w_down': mlp_w_down, 'final_norm': final_norm}
    W = _prep_weights(P)
    Bp, Tp, D = x_prompt.shape
    Bs, Ts, _ = x_sample.shape
    assert Ts == 1, "the sample group advances one token per sequence"
    states = (jnp.swapaxes(state_rglru_conv, 1, 2), state_rglru_h, jnp.swapaxes(state_conf_conv, 1, 2),
              state_mlstm_C, state_mlstm_n, state_mlstm_m)
    yp, ys, p_states, s_states = _run_trunk(x_prompt.reshape(Bp * Tp, D), x_sample.reshape(Bs, D),
                                            P, W, Bp, Tp, states)
    return (yp.reshape(Bp, Tp, D), ys.reshape(Bs, Ts, D), *p_states, *s_states)
```

```python
import functools
import math

import jax
import jax.numpy as jnp
from jax import lax
from jax.experimental import pallas as pl
from jax.experimental.pallas import tpu as pltpu

F32 = jnp.float32
BF16 = jnp.bfloat16
EPS = 1e-6
LRU_C = 8.0
ML_CHUNK = 128
GATE_BLOCK = 256
SUBLANES = 8
MIB = 1024 * 1024


def _cp(sem, vmem_mib):
    return pltpu.CompilerParams(dimension_semantics=sem, vmem_limit_bytes=vmem_mib * MIB)


def _tile(n, pref):
    return pref if n % pref == 0 else n


BF16_ROWS = 16
MM_ROWS_CAP = 1100
MLP_TF_F32 = 512
MLP_TF_BF16 = 1024
MLP_VMEM_MIB = 56
OUT_ROWS_CAP = 640


def _row_tile(m, cap):
    best = None
    for t in range(BF16_ROWS, min(m, cap) + 1, BF16_ROWS):
        if m % t == 0:
            best = t
    assert best is not None, (m, cap)
    return best


def _rms(x, g):
    return (x * lax.rsqrt(jnp.mean(x * x, axis=-1, keepdims=True) + EPS)) * g


def _sigmoid(x):
    return jax.nn.sigmoid(x)


def _softplus(z):
    return jnp.maximum(z, 0.0) + jnp.log1p(jnp.exp(-jnp.abs(z)))


def _log_sigmoid(x):
    return -_softplus(-x)


def _gelu_tanh(x):
    c = math.sqrt(2.0 / math.pi)
    return x * (0.5 * (1.0 + jnp.tanh(c * (x + 0.044715 * (x * x * x)))))


def _layernorm(x, g, b):
    mu = jnp.mean(x, axis=-1, keepdims=True)
    d = x - mu
    var = jnp.mean(d * d, axis=-1, keepdims=True)
    return (d * lax.rsqrt(var + EPS)) * g + b


def _dot(a, b):
    return jnp.dot(a, b, preferred_element_type=F32)


def _split3(x):
    hi = x.astype(BF16)
    r1 = x - hi.astype(F32)
    mid = r1.astype(BF16)
    lo = (r1 - mid.astype(F32)).astype(BF16)
    return hi, mid, lo


def _entry_kernel(*refs, has_acc):
    x_ref, g_ref = refs[:2]
    xo_ref, xn_ref = refs[2 + 2 * int(has_acc):]
    x = x_ref[...]
    xo_ref[...] = x
    xn_ref[...] = _rms(x, g_ref[...]).astype(xn_ref.dtype)


def entry_norm(x, g, m_all, row0, acc):
    M, D = x.shape
    tm = _tile(M, 512)
    assert row0 % tm == 0
    off = row0 // tm
    has_acc = acc is not None
    in_specs = [pl.BlockSpec((tm, D), lambda i: (i, 0)), pl.BlockSpec((1, D), lambda i: (0, 0))]
    args = [x, g.reshape(1, D)]
    if has_acc:
        in_specs += [pl.BlockSpec(memory_space=pl.ANY)] * 2
        args += list(acc)
    out_spec = pl.BlockSpec((tm, D), lambda i: (i + off, 0))
    return pl.pallas_call(
        functools.partial(_entry_kernel, has_acc=has_acc), grid=(M // tm,),
        in_specs=in_specs, out_specs=[out_spec, out_spec],
        out_shape=[jax.ShapeDtypeStruct((m_all, D), F32), jax.ShapeDtypeStruct((m_all, D), BF16)],
        input_output_aliases={2: 0, 3: 1} if has_acc else {},
        compiler_params=_cp(("parallel",), 32), name="entry_norm")(*args)


def _matmul_kernel(x_ref, w_ref, o_ref, wb_ref, *, transposed):
    @pl.when(pl.program_id(1) == 0)
    def _():
        wb_ref[...] = w_ref[...].astype(BF16)

    if transposed:
        y = lax.dot_general(x_ref[...], wb_ref[...], (((1,), (1,)), ((), ())), preferred_element_type=F32)
    else:
        y = _dot(x_ref[...], wb_ref[...])
    o_ref[...] = y.astype(o_ref.dtype)


def matmul(x, w, layer, n_cols, tn, transposed=False):
    M, K = x.shape
    tm = _row_tile(M, MM_ROWS_CAP)
    tn = _tile(n_cols, tn)
    if transposed:
        w_spec = pl.BlockSpec((None, tn, K), lambda j, i: (layer, j, 0))
        w_tile = (tn, K)
    else:
        w_spec = pl.BlockSpec((None, K, tn), lambda j, i: (layer, 0, j))
        w_tile = (K, tn)
    return pl.pallas_call(
        functools.partial(_matmul_kernel, transposed=transposed), grid=(n_cols // tn, M // tm),
        in_specs=[pl.BlockSpec((tm, K), lambda j, i: (i, 0)), w_spec],
        out_specs=pl.BlockSpec((tm, tn), lambda j, i: (i, j)),
        out_shape=jax.ShapeDtypeStruct((M, n_cols), F32),
        scratch_shapes=[pltpu.VMEM(w_tile, BF16)],
        compiler_params=_cp(("parallel", "arbitrary"), 48), name="in_proj")(x, w)


def _outproj_kernel(*refs, n_act):
    acts = refs[:n_act]
    w_ref, x_ref, g_ref, xo_ref, xn_ref = refs[n_act:]
    a = jnp.concatenate([a_ref[...].astype(BF16) for a_ref in acts], axis=1)
    x_new = x_ref[...] + _dot(a, w_ref[...])
    xo_ref[...] = x_new
    xn_ref[...] = _rms(x_new, g_ref[...]).astype(xn_ref.dtype)


def out_proj(acts, w, layer, x, g):
    M, D = x.shape
    K = w.shape[1]
    tm = _row_tile(M, OUT_ROWS_CAP)
    row = lambda i: (i, 0)
    fixed = lambda i: (0, 0)
    in_specs = [pl.BlockSpec((tm, a.shape[1]), row) for a in acts]
    in_specs += [pl.BlockSpec((None, K, D), lambda i: (layer, 0, 0), pipeline_mode=pl.Buffered(1)),
                 pl.BlockSpec((tm, D), row), pl.BlockSpec((1, D), fixed)]
    return pl.pallas_call(
        functools.partial(_outproj_kernel, n_act=len(acts)), grid=(M // tm,),
        in_specs=in_specs,
        out_specs=[pl.BlockSpec((tm, D), row), pl.BlockSpec((tm, D), row)],
        out_shape=[jax.ShapeDtypeStruct((M, D), F32), jax.ShapeDtypeStruct((M, D), BF16)],
        compiler_params=_cp(("parallel",), 52), name="out_proj")(*acts, w, x, g.reshape(1, D))


def _mlp_kernel(xn_ref, x_ref, wu_ref, wd_ref, g_ref, *rest, emit_x, emit_w, n_acc):
    rest = list(rest[n_acc:])
    xo_ref = rest.pop(0) if emit_x else rest.pop()
    xn_out_ref = rest.pop(0)
    f = pl.program_id(1)

    @pl.when(f == 0)
    def _():
        xo_ref[...] = x_ref[...]

    wu = wu_ref[...].astype(BF16)
    wd = wd_ref[...].astype(BF16)
    if emit_w:
        wu_o, wd_o = rest
        wu_o[...] = wu
        wd_o[...] = wd
    h = _dot(xn_ref[...], wu)
    h = jnp.square(jnp.maximum(h, 0.0)).astype(BF16)
    xo_ref[...] += _dot(h, wd)

    @pl.when(f == pl.num_programs(1) - 1)
    def _():
        xn_out_ref[...] = _rms(xo_ref[...], g_ref[...]).astype(xn_out_ref.dtype)


def mlp(xn, x, w_up, w_down, layer, g, out_dtype, row0, nrows, *, tf, emit_x=True, emit_w=False,
        acc=None, out_rows=None):
    D = x.shape[1]
    F = w_up.shape[-1]
    tm = _row_tile(nrows, MM_ROWS_CAP)
    assert row0 % tm == 0 and F % tf == 0
    assert not emit_w or nrows == tm
    off = row0 // tm
    out_off = off if out_rows is not None else 0
    m_out = out_rows if out_rows is not None else nrows
    row_in = lambda i, f: (i + off, 0)
    once = pl.Buffered(1)
    row_out = pl.BlockSpec((tm, D), lambda i, f: (i + out_off, 0), pipeline_mode=once)
    if layer is None:
        wu_spec = pl.BlockSpec((D, tf), lambda i, f: (0, f))
        wd_spec = pl.BlockSpec((tf, D), lambda i, f: (f, 0))
    else:
        wu_spec = pl.BlockSpec((None, D, tf), lambda i, f: (layer, 0, f))
        wd_spec = pl.BlockSpec((None, tf, D), lambda i, f: (layer, f, 0))
    in_specs = [pl.BlockSpec((tm, D), row_in), pl.BlockSpec((tm, D), row_in, pipeline_mode=once),
                wu_spec, wd_spec, pl.BlockSpec((1, D), lambda i, f: (0, 0))]
    args = [xn, x, w_up, w_down, g.reshape(1, D)]
    out_specs, out_shape = [], []
    if emit_x:
        out_specs.append(row_out)
        out_shape.append(jax.ShapeDtypeStruct((m_out, D), F32))
    out_specs.append(row_out)
    out_shape.append(jax.ShapeDtypeStruct((m_out, D), out_dtype))
    aliases = {}
    if acc is not None:
        assert emit_x and out_rows is not None
        in_specs += [pl.BlockSpec(memory_space=pl.ANY)] * 2
        args += list(acc)
        aliases = {5: 0, 6: 1}
    if emit_w:
        out_specs += [pl.BlockSpec((D, tf), lambda i, f: (0, f)), pl.BlockSpec((tf, D), lambda i, f: (f, 0))]
        out_shape += [jax.ShapeDtypeStruct((D, F), BF16), jax.ShapeDtypeStruct((F, D), BF16)]
    outs = pl.pallas_call(
        functools.partial(_mlp_kernel, emit_x=emit_x, emit_w=emit_w, n_acc=0 if acc is None else 2),
        grid=(nrows // tm, F // tf), in_specs=in_specs, out_specs=out_specs, out_shape=out_shape,
        scratch_shapes=[] if emit_x else [pltpu.VMEM((tm, D), F32)],
        input_output_aliases=aliases,
        compiler_params=_cp(("parallel", "arbitrary"), MLP_VMEM_MIB), name="mlp")(*args)
    return list(outs)


def mlp_all_rows(xn, x, w_up, w_down, layer, g, out_dtype):
    m_all = x.shape[0]
    tm = _row_tile(m_all, MM_ROWS_CAP)
    if tm == m_all:
        return mlp(xn, x, w_up, w_down, layer, g, out_dtype, 0, m_all, tf=MLP_TF_F32)
    x0, xn0, wu_b, wd_b = mlp(xn, x, w_up, w_down, layer, g, out_dtype, 0, tm, tf=MLP_TF_F32,
                              emit_w=True, out_rows=m_all)
    return mlp(xn, x, wu_b, wd_b, None, g, out_dtype, tm, m_all - tm, tf=MLP_TF_BF16,
               acc=(x0, xn0), out_rows=m_all)


def _rglru_gates(xc, wg_ref, ba, bx, lam):
    C = xc.shape[1]
    xcb = xc.astype(BF16)
    r_parts, i_parts = [], []
    for c in range(C // GATE_BLOCK):
        lo, hi = c * GATE_BLOCK, (c + 1) * GATE_BLOCK
        gates = _dot(xcb[:, lo:hi], wg_ref[c])
        r_parts.append(gates[:, :GATE_BLOCK])
        i_parts.append(gates[:, GATE_BLOCK:])
    r = _sigmoid(jnp.concatenate(r_parts, axis=1) + ba)
    ig = _sigmoid(jnp.concatenate(i_parts, axis=1) + bx)
    log_a = (-LRU_C * _softplus(-lam)) * r
    a = jnp.exp(log_a)
    u = jnp.sqrt(-jnp.tanh(log_a) * (a * a + 1.0)) * (ig * xc)
    return a, u


def _rglru_prompt_kernel(xr_ref, gr_ref, cw_ref, cb_ref, wg_ref, ba_ref, bx_ref, lam_ref,
                         y_ref, tail_ref, hlast_ref, xp_ref, a_ref, u_ref, h_ref):
    t = pl.program_id(1)
    tT, C = xr_ref.shape
    nk = cw_ref.shape[0]

    @pl.when(t == 0)
    def _():
        xp_ref[0:SUBLANES, :] = jnp.zeros((SUBLANES, C), F32)
        h_ref[...] = jnp.zeros_like(h_ref)

    x = xr_ref[...]
    xp_ref[SUBLANES:SUBLANES + tT, :] = x
    xc = cb_ref[...] + cw_ref[nk - 1:nk, :] * x
    for k in range(nk - 1):
        off = SUBLANES - (nk - 1) + k
        xc = xc + cw_ref[k:k + 1, :] * xp_ref[off:off + tT, :]
    xp_ref[0:SUBLANES, :] = x[tT - SUBLANES:tT, :]

    a, u = _rglru_gates(xc, wg_ref, ba_ref[...], bx_ref[...], lam_ref[...])
    a_ref[...] = a
    u_ref[...] = u

    rowi = lax.broadcasted_iota(jnp.int32, (SUBLANES, C), 0)

    def body(g, h):
        r0 = pl.multiple_of(g * SUBLANES, SUBLANES)
        a8 = a_ref[pl.ds(r0, SUBLANES), :]
        u8 = u_ref[pl.ds(r0, SUBLANES), :]
        for s in (1, 2, 4):
            a_sh = pltpu.roll(a8, s, 0)
            u_sh = pltpu.roll(u8, s, 0)
            m = rowi >= s
            u8 = jnp.where(m, a8 * u_sh + u8, u8)
            a8 = jnp.where(m, a8 * a_sh, a8)
        h8 = a8 * h + u8
        u_ref[pl.ds(r0, SUBLANES), :] = h8
        return h8[SUBLANES - 1:SUBLANES, :]

    h = lax.fori_loop(0, tT // SUBLANES, body, h_ref[...])
    h_ref[...] = h
    y_ref[...] = (u_ref[...] * _gelu_tanh(gr_ref[...])).astype(y_ref.dtype)

    @pl.when(t == pl.num_programs(1) - 1)
    def _():
        tail_ref[0] = x[tT - SUBLANES:tT, :]
        hlast_ref[0] = h


def rglru_prompt(proj, B, T, cw, cb, wg, ba, bx, lam):
    C = cw.shape[1]
    tT = _tile(T, 256)
    nT = T // tT
    fixed2 = lambda b, t: (0, 0)
    vec = pl.BlockSpec((1, C), fixed2)
    return pl.pallas_call(
        _rglru_prompt_kernel, grid=(B, nT),
        in_specs=[pl.BlockSpec((tT, C), lambda b, t: (b * nT + t, 0)),
                  pl.BlockSpec((tT, C), lambda b, t: (b * nT + t, 1)),
                  pl.BlockSpec(cw.shape, fixed2), vec,
                  pl.BlockSpec(wg.shape, lambda b, t: (0, 0, 0)), vec, vec, vec],
        out_specs=[pl.BlockSpec((tT, C), lambda b, t: (b * nT + t, 0)),
                   pl.BlockSpec((1, SUBLANES, C), lambda b, t: (b, 0, 0)),
                   pl.BlockSpec((1, 1, C), lambda b, t: (b, 0, 0))],
        out_shape=[jax.ShapeDtypeStruct((proj.shape[0], C), BF16),
                   jax.ShapeDtypeStruct((B, SUBLANES, C), F32),
                   jax.ShapeDtypeStruct((B, 1, C), F32)],
        scratch_shapes=[pltpu.VMEM((SUBLANES + tT, C), F32), pltpu.VMEM((tT, C), F32),
                        pltpu.VMEM((tT, C), F32), pltpu.VMEM((1, C), F32)],
        compiler_params=_cp(("parallel", "arbitrary"), 40), name="rglru_prompt",
    )(proj, proj, cw, cb.reshape(1, C), wg, ba.reshape(1, C), bx.reshape(1, C), lam.reshape(1, C))


CONF_STRIDE = 4
LANES = 128


def _conf_prompt_kernel(ac_ref, bc_ref, dw_ref, db_ref, lg_ref, lb_ref,
                        y_ref, tail_ref, gp_ref, yc_ref):
    t = pl.program_id(1)
    tT, C = ac_ref.shape
    nk = dw_ref.shape[0]
    pad = gp_ref.shape[1] - tT
    nslab = C // LANES
    span = CONF_STRIDE * SUBLANES

    @pl.when(t == 0)
    def _():
        gp_ref[:, 0:pad, :] = jnp.zeros((nslab, pad, LANES), F32)

    glu = ac_ref[...] * _sigmoid(bc_ref[...])
    for c in range(nslab):
        gp_ref[c, pad:pad + tT, :] = glu[:, c * LANES:(c + 1) * LANES]
    for c in range(nslab):
        lo = c * LANES
        for r0 in range(0, tT, span):
            accs = [jnp.broadcast_to(db_ref[:, lo:lo + LANES], (SUBLANES, LANES)) for _ in range(CONF_STRIDE)]
            for k in range(nk):
                w = dw_ref[k:k + 1, lo:lo + LANES]
                off = pad - (nk - 1) + k + r0
                for p in range(CONF_STRIDE):
                    accs[p] = accs[p] + gp_ref[c, pl.ds(off + p, SUBLANES, stride=CONF_STRIDE), :] * w
            for p in range(CONF_STRIDE):
                yc_ref[c, pl.ds(r0 + p, SUBLANES, stride=CONF_STRIDE), :] = accs[p]
    yc = jnp.concatenate([yc_ref[c] for c in range(nslab)], axis=1)
    yn = _layernorm(yc, lg_ref[...], lb_ref[...])
    y_ref[...] = (yn * _sigmoid(yn)).astype(y_ref.dtype)

    @pl.when(t == pl.num_programs(1) - 1)
    def _():
        tail_ref[0] = jnp.concatenate([gp_ref[c, tT:tT + pad, :] for c in range(nslab)], axis=1)

    gp_ref[:, 0:pad, :] = gp_ref[:, tT:tT + pad, :]


def conf_prompt(proj, B, T, dw, db, lg, lb):
    nk, C = dw.shape
    tT = _tile(T, 256)
    nT = T // tT
    pad = -(-(nk - 1) // SUBLANES) * SUBLANES
    assert tT % (CONF_STRIDE * SUBLANES) == 0 and C % LANES == 0
    fixed2 = lambda b, t: (0, 0)
    vec = pl.BlockSpec((1, C), fixed2)
    return pl.pallas_call(
        _conf_prompt_kernel, grid=(B, nT),
        in_specs=[pl.BlockSpec((tT, C), lambda b, t: (b * nT + t, 2)),
                  pl.BlockSpec((tT, C), lambda b, t: (b * nT + t, 3)),
                  pl.BlockSpec((nk, C), fixed2), vec, vec, vec],
        out_specs=[pl.BlockSpec((tT, C), lambda b, t: (b * nT + t, 0)),
                   pl.BlockSpec((1, pad, C), lambda b, t: (b, 0, 0))],
        out_shape=[jax.ShapeDtypeStruct((proj.shape[0], C), BF16), jax.ShapeDtypeStruct((B, pad, C), F32)],
        scratch_shapes=[pltpu.VMEM((C // LANES, pad + tT, LANES), F32),
                        pltpu.VMEM((C // LANES, tT, LANES), F32)],
        compiler_params=_cp(("parallel", "arbitrary"), 32), name="conf_prompt",
    )(proj, proj, dw, db.reshape(1, C), lg.reshape(1, C), lb.reshape(1, C))


def _ab_sample_kernel(*refs, has_acc):
    (proj_ref, rconv_ref, h0_ref, cconv_ref,
     cw_ref, cb_ref, wg_ref, ba_ref, bx_ref, lam_ref, dw_ref, db_ref, lg_ref, lb_ref) = refs[:14]
    yr_ref, yc_ref, rconv_o, h_o, cconv_o = refs[14 + int(has_acc):]
    C = h0_ref.shape[1]
    nk = cw_ref.shape[0]
    nd = dw_ref.shape[0]
    x_r = proj_ref[:, 0:C]
    g_r = proj_ref[:, C:2 * C]
    glu = proj_ref[:, 2 * C:3 * C] * _sigmoid(proj_ref[:, 3 * C:4 * C])

    xc = cb_ref[...] + cw_ref[nk - 1:nk, :] * x_r
    for k in range(nk - 1):
        xc = xc + cw_ref[k:k + 1, :] * rconv_ref[k]
    for k in range(nk - 2):
        rconv_o[k] = rconv_ref[k + 1]
    rconv_o[nk - 2] = x_r

    yc = db_ref[...] + dw_ref[nd - 1:nd, :] * glu
    for k in range(nd - 1):
        yc = yc + dw_ref[k:k + 1, :] * cconv_ref[k]
    for k in range(nd - 2):
        cconv_o[k] = cconv_ref[k + 1]
    cconv_o[nd - 2] = glu

    a, u = _rglru_gates(xc, wg_ref, ba_ref[...], bx_ref[...], lam_ref[...])
    h = a * h0_ref[...] + u
    h_o[...] = h
    yr_ref[...] = h * _gelu_tanh(g_r)
    yn = _layernorm(yc, lg_ref[...], lb_ref[...])
    yc_ref[...] = yn * _sigmoid(yn)


def ab_sample(proj, row0, rconv_t, h0, cconv_t, layer, cconv_acc, cw, cb, wg, ba, bx, lam, dw, db, lg, lb):
    _, B, C = h0.shape
    bt = _tile(B, 32)
    row = lambda i: (i, 0)
    fixed = lambda i: (0, 0)
    vec = pl.BlockSpec((1, C), fixed)
    nr, nc = rconv_t.shape[1], cconv_t.shape[1]
    st_c = pl.BlockSpec((None, nc, bt, C), lambda i: (layer, 0, i, 0))
    act = pl.BlockSpec((bt, C), row)
    has_acc = cconv_acc is not None
    assert row0 % bt == 0
    off = row0 // bt
    in_specs = [pl.BlockSpec((bt, proj.shape[1]), lambda i: (i + off, 0)),
                pl.BlockSpec((None, nr, bt, C), lambda i: (layer, 0, i, 0)),
                pl.BlockSpec((None, bt, C), lambda i: (layer, i, 0)), st_c,
                pl.BlockSpec(cw.shape, fixed), vec, pl.BlockSpec(wg.shape, lambda i: (0, 0, 0)),
                vec, vec, vec, pl.BlockSpec(dw.shape, fixed), vec, vec, vec]
    args = [proj, rconv_t, h0, cconv_t, cw, cb.reshape(1, C), wg, ba.reshape(1, C), bx.reshape(1, C),
            lam.reshape(1, C), dw, db.reshape(1, C), lg.reshape(1, C), lb.reshape(1, C)]
    if has_acc:
        in_specs.append(pl.BlockSpec(memory_space=pl.ANY))
        args.append(cconv_acc)
    return pl.pallas_call(
        functools.partial(_ab_sample_kernel, has_acc=has_acc), grid=(B // bt,),
        in_specs=in_specs,
        out_specs=[act, act, pl.BlockSpec((nr, bt, C), lambda i: (0, i, 0)), act, st_c],
        out_shape=[jax.ShapeDtypeStruct((B, C), F32), jax.ShapeDtypeStruct((B, C), F32),
                   jax.ShapeDtypeStruct((nr, B, C), F32), jax.ShapeDtypeStruct((B, C), F32),
                   jax.ShapeDtypeStruct(cconv_t.shape, F32)],
        input_output_aliases={len(args) - 1: 4} if has_acc else {},
        compiler_params=_cp(("parallel",), 40), name="ab_sample",
    )(*args)


def _mlstm_prompt_kernel(q_ref, k_ref, v_ref, o_ref, g_ref, bif_ref, hng_ref,
                         y_ref, C_ref, n_ref, m_ref, *, heads):
    c = pl.program_id(1)
    L = q_ref.shape[0]
    H = heads
    DK = q_ref.shape[1] // H
    DV = v_ref.shape[1] // H
    scale = DK ** -0.5

    @pl.when(c == 0)
    def _():
        C_ref[...] = jnp.zeros_like(C_ref)
        n_ref[...] = jnp.zeros_like(n_ref)
        m_ref[...] = jnp.zeros_like(m_ref)

    ti = lax.broadcasted_iota(jnp.int32, (L, L), 0)
    si = lax.broadcasted_iota(jnp.int32, (L, L), 1)
    causal = si <= ti
    tri_lo = causal.astype(BF16)
    tri_up = (ti <= si).astype(BF16)

    gates = g_ref[...] + bif_ref[...]
    lf = _log_sigmoid(gates)
    gates_t = gates.T
    lf_t = lf.T
    hi, mid, lo = _split3(lf)
    b_cols = (_dot(tri_lo, lo) + _dot(tri_lo, mid)) + _dot(tri_lo, hi)
    hi, mid, lo = _split3(lf_t)
    b_rows = (_dot(lo, tri_up) + _dot(mid, tri_up)) + _dot(hi, tri_up)

    m_all = m_ref[0]
    n_all = n_ref[0]
    hs = range(H)
    dk = lambda h: slice(h * DK, (h + 1) * DK)
    dv = lambda h: slice(h * DV, (h + 1) * DV)
    ig_col = [gates[:, h:h + 1] for h in hs]
    ig_row = [gates_t[h:h + 1, :] for h in hs]
    b_col = [b_cols[:, H + h:H + h + 1] for h in hs]
    b_row = [b_rows[H + h:H + h + 1, :] for h in hs]
    m_prev = [m_all[h:h + 1, 0:1] for h in hs]
    n_row = [n_all[h:h + 1, :] for h in hs]

    qs = [q_ref[:, dk(h)] * scale for h in hs]
    qb = [qs[h].astype(BF16) for h in hs]
    kb = [k_ref[:, dk(h)].astype(BF16) for h in hs]
    vb = [v_ref[:, dv(h)].astype(BF16) for h in hs]
    dmat = [jnp.where(causal, b_col[h] - b_row[h] + ig_row[h], -jnp.inf) for h in hs]
    inter = [b_col[h] + m_prev[h] for h in hs]
    m_t = [jnp.maximum(inter[h], jnp.max(dmat[h], axis=-1, keepdims=True)) for h in hs]
    qk = [lax.dot_general(qb[h], kb[h], (((1,), (1,)), ((), ())), preferred_element_type=F32) for h in hs]
    s = [qk[h] * jnp.exp(dmat[h] - m_t[h]) for h in hs]
    scale_in = [jnp.exp(inter[h] - m_t[h]) for h in hs]
    qC = [_dot(qb[h], C_ref[0, h].astype(BF16)) for h in hs]
    num = [_dot(s[h].astype(BF16), vb[h]) + scale_in[h] * qC[h] for h in hs]
    den = [jnp.sum(s[h], axis=-1, keepdims=True)
           + scale_in[h] * jnp.sum(qs[h] * n_row[h], axis=-1, keepdims=True) for h in hs]
    hh = [num[h] / jnp.maximum(jnp.abs(den[h]), jnp.exp(-m_t[h])) for h in hs]
    for h in hs:
        hn = hh[h] * lax.rsqrt(jnp.mean(hh[h] * hh[h], axis=-1, keepdims=True) + EPS)
        y = (hn * hng_ref[:, dv(h)]) * _sigmoid(o_ref[:, dv(h)])
        y_ref[:, dv(h)] = y.astype(y_ref.dtype)
    b_last = [b_col[h][L - 1:L, :] for h in hs]
    g_row = [b_last[h] - b_row[h] + ig_row[h] for h in hs]
    g_col = [b_last[h] - b_col[h] + ig_col[h] for h in hs]
    m_new = [jnp.maximum(b_last[h] + m_prev[h], jnp.max(g_row[h], axis=-1, keepdims=True)) for h in hs]
    decay = [jnp.exp(b_last[h] + m_prev[h] - m_new[h]) for h in hs]
    kw = [k_ref[:, dk(h)] * jnp.exp(g_col[h] - m_new[h]) for h in hs]
    for h in hs:
        C_ref[0, h] = decay[h] * C_ref[0, h] + lax.dot_general(
            kw[h].astype(BF16), vb[h], (((0,), (0,)), ((), ())), preferred_element_type=F32)
    n_ref[0] = jnp.concatenate([decay[h] * n_row[h] + jnp.sum(kw[h], axis=0, keepdims=True) for h in hs], axis=0)
    m_ref[0] = jnp.concatenate([jnp.broadcast_to(m_new[h], (1, m_ref.shape[2])) for h in hs], axis=0)


def mlstm_prompt(proj, gates, bif_row, hng, B, T, heads):
    QK = proj.shape[1] // 6
    V = 2 * QK
    H = heads
    DK, DV = QK // H, V // H
    L = ML_CHUNK if T % ML_CHUNK == 0 else T
    nc = T // L
    rowmap = lambda j: (lambda b, c: (b * nc + c, j))
    fixed2 = lambda b, c: (0, 0)
    return pl.pallas_call(
        functools.partial(_mlstm_prompt_kernel, heads=H), grid=(B, nc),
        in_specs=[pl.BlockSpec((L, QK), rowmap(0)), pl.BlockSpec((L, QK), rowmap(1)),
                  pl.BlockSpec((L, V), rowmap(1)), pl.BlockSpec((L, V), rowmap(2)),
                  pl.BlockSpec((L, LANES), rowmap(0)), pl.BlockSpec((1, LANES), fixed2),
                  pl.BlockSpec((1, V), fixed2)],
        out_specs=[pl.BlockSpec((L, V), rowmap(0)),
                   pl.BlockSpec((1, H, DK, DV), lambda b, c: (b, 0, 0, 0)),
                   pl.BlockSpec((1, H, DK), lambda b, c: (b, 0, 0)),
                   pl.BlockSpec((1, H, LANES), lambda b, c: (b, 0, 0))],
        out_shape=[jax.ShapeDtypeStruct((proj.shape[0], V), BF16),
                   jax.ShapeDtypeStruct((B, H, DK, DV), F32),
                   jax.ShapeDtypeStruct((B, H, DK), F32),
                   jax.ShapeDtypeStruct((B, H, LANES), F32)],
        compiler_params=_cp(("parallel", "arbitrary"), 40), name="mlstm_prompt",
    )(proj, proj, proj, proj, gates, bif_row, hng.reshape(1, V))


def _mlstm_gate_terms(ig, fg, m):
    lf = _log_sigmoid(fg)
    inter = lf + m
    m_t = jnp.maximum(inter, ig)
    return m_t, jnp.exp(ig - m_t), jnp.exp(inter - m_t)


def _mlstm_sample_kernel(*refs, has_acc):
    (qr_ref, kr_ref, v_ref, o_ref, gr_ref, gc_ref, bifr_ref, bifc_ref,
     hng_ref, C_ref, n_ref) = refs[:11]
    y_ref, Co_ref, no_ref, mo_ref = refs[11 + int(has_acc):]
    bb, H, DK = qr_ref.shape
    scale = DK ** -0.5
    bifr = bifr_ref[...]
    bifc = bifc_ref[...]
    for j in range(bb):
        gr = gr_ref[j]
        gc = gc_ref[j]
        mt_r, es_r, sc_r = _mlstm_gate_terms(gr[:, 0:H] + bifr[:, 0:H], gr[:, H:2 * H] + bifr[:, H:2 * H],
                                             gr[:, 2 * H:3 * H])
        mt_c, es_c, sc_c = _mlstm_gate_terms(gc[0:H, :] + bifc[0:H, :], gc[H:2 * H, :] + bifc[H:2 * H, :],
                                             gc[2 * H:3 * H, :])
        qr = qr_ref[j] * scale
        kr = kr_ref[j]
        n = n_ref[j]
        v = v_ref[j]
        qk = jnp.sum(qr * kr, axis=1, keepdims=True)
        qn = jnp.sum(qr * n, axis=1, keepdims=True)
        qt = qr.T
        kts = kr.T * es_r
        rows = []
        for h in range(H):
            Ch = C_ref[j, h]
            rows.append(jnp.sum(Ch * qt[:, h:h + 1], axis=0, keepdims=True))
            Co_ref[j, h] = sc_r[:, h:h + 1] * Ch + kts[:, h:h + 1] * v[h:h + 1, :]
        qC = jnp.concatenate(rows, axis=0)
        s = qk * es_c
        num = s * v + sc_c * qC
        den = s + sc_c * qn
        hh = num / jnp.maximum(jnp.abs(den), jnp.exp(-mt_c))
        hn = hh * lax.rsqrt(jnp.mean(hh * hh, axis=-1, keepdims=True) + EPS)
        y_ref[j] = (hn * hng_ref[...]) * _sigmoid(o_ref[j])
        no_ref[j] = sc_c * n + es_c * kr
        mo_ref[j] = mt_r


def mlstm_sample(q, k, v, o, gates, m, bif, hng, C, n, layer, C_acc):
    B, H, DK = q.shape
    DV = v.shape[2]
    bb = _tile(B, 8)
    gm = jnp.concatenate([gates, m], axis=1)
    b3 = lambda i: (i, 0, 0)
    fixed = lambda i: (0, 0)
    blk = lambda a: pl.BlockSpec((bb,) + a.shape[1:], b3)
    C_spec = pl.BlockSpec((None, bb, H, DK, DV), lambda i: (layer, i, 0, 0, 0))
    has_acc = C_acc is not None
    in_specs = [blk(q), blk(k), blk(v), blk(o),
                pl.BlockSpec((bb, 1, 3 * H), b3), pl.BlockSpec((bb, 3 * H, 1), b3),
                pl.BlockSpec((1, 2 * H), fixed), pl.BlockSpec((2 * H, 1), fixed),
                pl.BlockSpec((H, DV), fixed), C_spec,
                pl.BlockSpec((None, bb, H, DK), lambda i: (layer, i, 0, 0))]
    args = [q, k, v, o, gm.reshape(B, 1, 3 * H), gm.reshape(B, 3 * H, 1),
            bif.reshape(1, 2 * H), bif.reshape(2 * H, 1), hng.reshape(H, DV), C, n]
    if has_acc:
        in_specs.append(pl.BlockSpec(memory_space=pl.ANY))
        args.append(C_acc)
    return pl.pallas_call(
        functools.partial(_mlstm_sample_kernel, has_acc=has_acc), grid=(B // bb,),
        in_specs=in_specs,
        out_specs=[blk(v), C_spec, pl.BlockSpec((bb, H, DK), b3), pl.BlockSpec((bb, 1, H), b3)],
        out_shape=[jax.ShapeDtypeStruct((B, H, DV), F32), jax.ShapeDtypeStruct(C.shape, F32),
                   jax.ShapeDtypeStruct((B, H, DK), F32), jax.ShapeDtypeStruct((B, 1, H), F32)],
        input_output_aliases={len(args) - 1: 1} if has_acc else {},
        compiler_params=_cp(("parallel",), 48), name="mlstm_sample",
    )(*args)


def _blockdiag(w, per):
    H, d, _ = w.shape
    w4 = w.reshape(H // per, per, d, d)
    eye = jnp.eye(per, dtype=w.dtype)
    return (w4[:, :, :, None, :] * eye[None, :, None, :, None]).reshape(H // per, per * d, per * d)


def _prep_weights(P):
    W = {}
    W['ab_w_out'] = P['ab_w_out'].astype(BF16)
    W['ml_w_out'] = P['ml_w_out'].astype(BF16)
    heads = P['ml_b_if'].shape[1] // 2
    n_main = P['ml_w_in'].shape[2] - 2 * heads
    W['ml_w_in_t'] = jnp.swapaxes(P['ml_w_in'], 1, 2)
    W['ml_w_gate'] = jnp.pad(P['ml_w_in'][:, :, n_main:], ((0, 0), (0, 0), (0, LANES - 2 * heads)))
    W['ml_bif_row'] = jnp.pad(P['ml_b_if'], ((0, 0), (0, LANES - 2 * heads)))[:, None, :]
    d = P['rglru_wa'].shape[2]
    per = GATE_BLOCK // d
    wa = jax.vmap(lambda w: _blockdiag(w, per))(P['rglru_wa'])
    wx = jax.vmap(lambda w: _blockdiag(w, per))(P['rglru_wx'])
    W['rglru_wg'] = jnp.concatenate([wa, wx], axis=-1).astype(BF16)
    return W


def _set_rows(y_all, y_rows, row0):
    return lax.dynamic_update_slice(y_all, y_rows.astype(y_all.dtype), (row0, 0))


def _run_trunk(xp, xs, P, W, Bp, Tp, states):
    depth = P['mlp_norm'].shape[0]
    heads = P['ml_b_if'].shape[1] // 2
    n_main = P['ml_w_in'].shape[2] - 2 * heads
    Mp, Bs = xp.shape[0], xs.shape[0]
    m_all = Mp + Bs
    p_rconv, p_rh, p_cconv, p_C, p_n, p_m = [], [], [], [], [], []
    s_rconv, s_rh, s_n, s_m = [], [], [], []
    cconv_acc = None
    C_acc = None
    x, xn = entry_norm(xs, P['ab_norm'][0], m_all, Mp, entry_norm(xp, P['ab_norm'][0], m_all, 0, None))
    for l in range(depth):
        i = l // 2
        if l % 2 == 0:
            proj = matmul(xn, P['ab_w_in'], i, P['ab_w_in'].shape[2], 1024)
            rg = (P['rglru_conv_w'][i], P['rglru_conv_b'][i], W['rglru_wg'][i], P['rglru_ba'][i],
                  P['rglru_bx'][i], P['rglru_lambda'][i])
            cf = (P['conf_dw_w'][i], P['conf_dw_b'][i], P['conf_ln_g'][i], P['conf_ln_b'][i])
            y_r, tail_r, h_last = rglru_prompt(proj, Bp, Tp, *rg)
            y_c, tail_c = conf_prompt(proj, Bp, Tp, *cf)
            nr = P['rglru_conv_w'].shape[1] - 1
            ncf = P['conf_dw_w'].shape[1] - 1
            p_rconv.append(tail_r[:, tail_r.shape[1] - nr:])
            p_rh.append(h_last[:, 0])
            p_cconv.append(tail_c[:, tail_c.shape[1] - ncf:])
            ys_r, ys_c, rconv_new, h_new, cconv_acc = ab_sample(proj, Mp, states[0], states[1], states[2], i,
                                                                cconv_acc, *rg, *cf)
            s_rconv.append(rconv_new); s_rh.append(h_new)
            acts = [_set_rows(y_r, ys_r, Mp), _set_rows(y_c, ys_c, Mp)]
            x, xn = out_proj(acts, W['ab_w_out'], i, x, P['mlp_norm'][l])
        else:
            proj = matmul(xn, W['ml_w_in_t'], i, n_main, 1024, transposed=True)
            gates = matmul(xn, W['ml_w_gate'], i, LANES, LANES)
            hng = P['ml_hnorm_g'][i]
            y, C, n, m = mlstm_prompt(proj, gates, W['ml_bif_row'][i], hng, Bp, Tp, heads)
            p_C.append(C); p_n.append(n); p_m.append(m[:, :, 0])
            QK = n_main // 6
            V = 2 * QK
            ps = proj[Mp:]
            q = ps[:, :QK].reshape(Bs, heads, QK // heads)
            k = ps[:, QK:2 * QK].reshape(Bs, heads, QK // heads)
            v = ps[:, 2 * QK:2 * QK + V].reshape(Bs, heads, V // heads)
            o = ps[:, 2 * QK + V:].reshape(Bs, heads, V // heads)
            ys, C_acc, n, m = mlstm_sample(q, k, v, o, gates[Mp:, :2 * heads], states[5][i], P['ml_b_if'][i],
                                           hng, states[3], states[4], i, C_acc)
            s_n.append(n); s_m.append(m[:, 0])
            x, xn = out_proj([_set_rows(y, ys.reshape(Bs, V), Mp)], W['ml_w_out'], i, x, P['mlp_norm'][l])
        if l < depth - 1:
            g_next = P['ab_norm'][(l + 1) // 2] if (l + 1) % 2 == 0 else P['ml_norm'][(l + 1) // 2]
            x, xn = mlp_all_rows(xn, x, P['mlp_w_up'], P['mlp_w_down'], l, g_next, BF16)
    last = depth - 1
    ys, wu_b, wd_b = mlp(xn, x, P['mlp_w_up'], P['mlp_w_down'], last, P['final_norm'], F32, Mp, Bs,
                         tf=MLP_TF_F32, emit_x=False, emit_w=True)
    (yp,) = mlp(xn, x, wu_b, wd_b, None, P['final_norm'], F32, 0, Mp, tf=MLP_TF_BF16, emit_x=False)
    prompt_out = (jnp.stack(p_rconv), jnp.stack(p_rh), jnp.stack(p_cconv),
                  jnp.stack(p_C), jnp.stack(p_n), jnp.stack(p_m))
    sample_out = (jnp.swapaxes(jnp.stack(s_rconv), 1, 2), jnp.stack(s_rh), jnp.swapaxes(cconv_acc, 1, 2),
                  C_acc, jnp.stack(s_n), jnp.stack(s_m))
    return yp, ys, prompt_out, sample_out


def kernel(x_prompt, x_sample, state_rglru_conv, state_rglru_h, state_conf_conv, state_mlstm_C, state_mlstm_n, state_mlstm_m, ab_norm, ab_w_in, rglru_conv_w, rglru_conv_b, rglru_wa, rglru_ba, rglru_wx, rglru_bx, rglru_lambda, conf_dw_w, conf_dw_b, conf_ln_g, conf_ln_b, ab_w_out, ml_norm, ml_w_in, ml_b_if, ml_hnorm_g, ml_w_out, mlp_norm, mlp_w_up, mlp_w_down, final_norm):
    P = {'ab_norm': ab_norm, 'ab_w_in': ab_w_in, 'rglru_conv_w': rglru_conv_w, 'rglru_conv_b': rglru_conv_b,
         'rglru_wa': rglru_wa, 'rglru_ba': rglru_ba, 'rglru_wx': rglru_wx, 'rglru_bx': rglru_bx,
         'rglru_lambda': rglru_lambda, 'conf_dw_w': conf_dw_w, 'conf_dw_b': conf_dw_b,
         'conf_ln_g': conf_ln_g, 'conf_ln_b': conf_ln_b, 'ab_w_out': ab_w_out, 'ml_norm': ml_norm,
         'ml_w_in': ml_w_in, 'ml_b_if': ml_b_if, 'ml_hnorm_g': ml_hnorm_g, 'ml_w_out': ml_w_out,
         'mlp_norm': mlp_norm, 'mlp_w_up': mlp_w_up, 'mlp_w_down': mlp_w_down, 'final_norm': final_norm}
    W = _prep_weights(P)
    Bp, Tp, D = x_prompt.shape
    Bs, Ts, _ = x_sample.shape
    assert Ts == 1, "the sample group advances one token per sequence"
    states = (jnp.swapaxes(state_rglru_conv, 1, 2), state_rglru_h, jnp.swapaxes(state_conf_conv, 1, 2),
              state_mlstm_C, state_mlstm_n, state_mlstm_m)
    yp, ys, p_states, s_states = _run_trunk(x_prompt.reshape(Bp * Tp, D), x_sample.reshape(Bs, D),
                                            P, W, Bp, Tp, states)
    return (yp.reshape(Bp, Tp, D), ys.reshape(Bs, Ts, D), *p_states, *s_states)
```

```python
import functools
import math

import jax
import jax.numpy as jnp
from jax import lax
from jax.experimental import pallas as pl
from jax.experimental.pallas import tpu as pltpu

F32 = jnp.float32
BF16 = jnp.bfloat16
EPS = 1e-6
LRU_C = 8.0
ML_CHUNK = 128
GATE_BLOCK = 256
SUBLANES = 8
MIB = 1024 * 1024


def _cp(sem, vmem_mib):
    return pltpu.CompilerParams(dimension_semantics=sem, vmem_limit_bytes=vmem_mib * MIB)


def _tile(n, pref):
    return pref if n % pref == 0 else n


BF16_ROWS = 16
MM_ROWS_CAP = 1100
MLP_TF_F32 = 512
MLP_TF_BF16 = 1024
MLP_VMEM_MIB = 56
OUT_ROWS_CAP = 640


def _row_tile(m, cap):
    best = None
    for t in range(BF16_ROWS, min(m, cap) + 1, BF16_ROWS):
        if m % t == 0:
            best = t
    assert best is not None, (m, cap)
    return best


def _rms(x, g):
    return (x * lax.rsqrt(jnp.mean(x * x, axis=-1, keepdims=True) + EPS)) * g


def _sigmoid(x):
    return jax.nn.sigmoid(x)


def _softplus(z):
    return jnp.maximum(z, 0.0) + jnp.log1p(jnp.exp(-jnp.abs(z)))


def _log_sigmoid(x):
    return -_softplus(-x)


def _gelu_tanh(x):
    c = math.sqrt(2.0 / math.pi)
    return x * (0.5 * (1.0 + jnp.tanh(c * (x + 0.044715 * (x * x * x)))))


def _layernorm(x, g, b):
    mu = jnp.mean(x, axis=-1, keepdims=True)
    d = x - mu
    var = jnp.mean(d * d, axis=-1, keepdims=True)
    return (d * lax.rsqrt(var + EPS)) * g + b


def _dot(a, b):
    return jnp.dot(a, b, preferred_element_type=F32)


def _split3(x):
    hi = x.astype(BF16)
    r1 = x - hi.astype(F32)
    mid = r1.astype(BF16)
    lo = (r1 - mid.astype(F32)).astype(BF16)
    return hi, mid, lo


def _entry_kernel(*refs, has_acc):
    x_ref, g_ref = refs[:2]
    xo_ref, xn_ref = refs[2 + 2 * int(has_acc):]
    x = x_ref[...]
    xo_ref[...] = x
    xn_ref[...] = _rms(x, g_ref[...]).astype(xn_ref.dtype)


def entry_norm(x, g, m_all, row0, acc):
    M, D = x.shape
    tm = _tile(M, 512)
    assert row0 % tm == 0
    off = row0 // tm
    has_acc = acc is not None
    in_specs = [pl.BlockSpec((tm, D), lambda i: (i, 0)), pl.BlockSpec((1, D), lambda i: (0, 0))]
    args = [x, g.reshape(1, D)]
    if has_acc:
        in_specs += [pl.BlockSpec(memory_space=pl.ANY)] * 2
        args += list(acc)
    out_spec = pl.BlockSpec((tm, D), lambda i: (i + off, 0))
    return pl.pallas_call(
        functools.partial(_entry_kernel, has_acc=has_acc), grid=(M // tm,),
        in_specs=in_specs, out_specs=[out_spec, out_spec],
        out_shape=[jax.ShapeDtypeStruct((m_all, D), F32), jax.ShapeDtypeStruct((m_all, D), BF16)],
        input_output_aliases={2: 0, 3: 1} if has_acc else {},
        compiler_params=_cp(("parallel",), 32), name="entry_norm")(*args)


def _matmul_kernel(x_ref, w_ref, o_ref, wb_ref, *, transposed):
    @pl.when(pl.program_id(1) == 0)
    def _():
        wb_ref[...] = w_ref[...].astype(BF16)

    if transposed:
        y = lax.dot_general(x_ref[...], wb_ref[...], (((1,), (1,)), ((), ())), preferred_element_type=F32)
    else:
        y = _dot(x_ref[...], wb_ref[...])
    o_ref[...] = y.astype(o_ref.dtype)


def matmul(x, w, layer, n_cols, tn, transposed=False):
    M, K = x.shape
    tm = _row_tile(M, MM_ROWS_CAP)
    tn = _tile(n_cols, tn)
    if transposed:
        w_spec = pl.BlockSpec((None, tn, K), lambda j, i: (layer, j, 0))
        w_tile = (tn, K)
    else:
        w_spec = pl.BlockSpec((None, K, tn), lambda j, i: (layer, 0, j))
        w_tile = (K, tn)
    return pl.pallas_call(
        functools.partial(_matmul_kernel, transposed=transposed), grid=(n_cols // tn, M // tm),
        in_specs=[pl.BlockSpec((tm, K), lambda j, i: (i, 0)), w_spec],
        out_specs=pl.BlockSpec((tm, tn), lambda j, i: (i, j)),
        out_shape=jax.ShapeDtypeStruct((M, n_cols), F32),
        scratch_shapes=[pltpu.VMEM(w_tile, BF16)],
        compiler_params=_cp(("parallel", "arbitrary"), 48), name="in_proj")(x, w)


def _outproj_kernel(*refs, n_act):
    acts = refs[:n_act]
    w_ref, x_ref, g_ref, xo_ref, xn_ref = refs[n_act:]
    a = jnp.concatenate([a_ref[...].astype(BF16) for a_ref in acts], axis=1)
    x_new = x_ref[...] + _dot(a, w_ref[...])
    xo_ref[...] = x_new
    xn_ref[...] = _rms(x_new, g_ref[...]).astype(xn_ref.dtype)


def out_proj(acts, w, layer, x, g):
    M, D = x.shape
    K = w.shape[1]
    tm = _row_tile(M, OUT_ROWS_CAP)
    row = lambda i: (i, 0)
    fixed = lambda i: (0, 0)
    in_specs = [pl.BlockSpec((tm, a.shape[1]), row) for a in acts]
    in_specs += [pl.BlockSpec((None, K, D), lambda i: (layer, 0, 0), pipeline_mode=pl.Buffered(1)),
                 pl.BlockSpec((tm, D), row), pl.BlockSpec((1, D), fixed)]
    return pl.pallas_call(
        functools.partial(_outproj_kernel, n_act=len(acts)), grid=(M // tm,),
        in_specs=in_specs,
        out_specs=[pl.BlockSpec((tm, D), row), pl.BlockSpec((tm, D), row)],
        out_shape=[jax.ShapeDtypeStruct((M, D), F32), jax.ShapeDtypeStruct((M, D), BF16)],
        compiler_params=_cp(("parallel",), 52), name="out_proj")(*acts, w, x, g.reshape(1, D))


def _mlp_kernel(xn_ref, x_ref, wu_ref, wd_ref, g_ref, *rest, emit_x, emit_w, n_acc):
    rest = list(rest[n_acc:])
    xo_ref = rest.pop(0) if emit_x else rest.pop()
    xn_out_ref = rest.pop(0)
    f = pl.program_id(1)

    @pl.when(f == 0)
    def _():
        xo_ref[...] = x_ref[...]

    wu = wu_ref[...].astype(BF16)
    wd = wd_ref[...].astype(BF16)
    if emit_w:
        wu_o, wd_o = rest
        wu_o[...] = wu
        wd_o[...] = wd
    h = _dot(xn_ref[...], wu)
    h = jnp.square(jnp.maximum(h, 0.0)).astype(BF16)
    xo_ref[...] += _dot(h, wd)

    @pl.when(f == pl.num_programs(1) - 1)
    def _():
        xn_out_ref[...] = _rms(xo_ref[...], g_ref[...]).astype(xn_out_ref.dtype)


def mlp(xn, x, w_up, w_down, layer, g, out_dtype, row0, nrows, *, tf, emit_x=True, emit_w=False,
        acc=None, out_rows=None, tm=None):
    D = x.shape[1]
    F = w_up.shape[-1]
    tm = _row_tile(nrows, MM_ROWS_CAP) if tm is None else tm
    assert row0 % tm == 0 and nrows % tm == 0 and F % tf == 0
    assert not emit_w or nrows == tm
    off = row0 // tm
    out_off = off if out_rows is not None else 0
    m_out = out_rows if out_rows is not None else nrows
    row_in = lambda i, f: (i + off, 0)
    once = pl.Buffered(1)
    row_out = pl.BlockSpec((tm, D), lambda i, f: (i + out_off, 0), pipeline_mode=once)
    if layer is None:
        wu_spec = pl.BlockSpec((D, tf), lambda i, f: (0, f))
        wd_spec = pl.BlockSpec((tf, D), lambda i, f: (f, 0))
    else:
        wu_spec = pl.BlockSpec((None, D, tf), lambda i, f: (layer, 0, f))
        wd_spec = pl.BlockSpec((None, tf, D), lambda i, f: (layer, f, 0))
    in_specs = [pl.BlockSpec((tm, D), row_in), pl.BlockSpec((tm, D), row_in, pipeline_mode=once),
                wu_spec, wd_spec, pl.BlockSpec((1, D), lambda i, f: (0, 0))]
    args = [xn, x, w_up, w_down, g.reshape(1, D)]
    out_specs, out_shape = [], []
    if emit_x:
        out_specs.append(row_out)
        out_shape.append(jax.ShapeDtypeStruct((m_out, D), F32))
    out_specs.append(row_out)
    out_shape.append(jax.ShapeDtypeStruct((m_out, D), out_dtype))
    aliases = {}
    if acc is not None:
        assert out_rows is not None and len(acc) == len(out_shape)
        in_specs += [pl.BlockSpec(memory_space=pl.ANY)] * len(acc)
        aliases = {len(args) + k: k for k in range(len(acc))}
        args += list(acc)
    if emit_w:
        out_specs += [pl.BlockSpec((D, tf), lambda i, f: (0, f)), pl.BlockSpec((tf, D), lambda i, f: (f, 0))]
        out_shape += [jax.ShapeDtypeStruct((D, F), BF16), jax.ShapeDtypeStruct((F, D), BF16)]
    outs = pl.pallas_call(
        functools.partial(_mlp_kernel, emit_x=emit_x, emit_w=emit_w, n_acc=0 if acc is None else len(acc)),
        grid=(nrows // tm, F // tf), in_specs=in_specs, out_specs=out_specs, out_shape=out_shape,
        scratch_shapes=[] if emit_x else [pltpu.VMEM((tm, D), F32)],
        input_output_aliases=aliases,
        compiler_params=_cp(("parallel", "arbitrary"), MLP_VMEM_MIB), name="mlp")(*args)
    return list(outs)


def mlp_all_rows(xn, x, w_up, w_down, layer, g, out_dtype):
    m_all = x.shape[0]
    tm = _row_tile(m_all, MM_ROWS_CAP)
    if tm == m_all:
        return mlp(xn, x, w_up, w_down, layer, g, out_dtype, 0, m_all, tf=MLP_TF_F32)
    x0, xn0, wu_b, wd_b = mlp(xn, x, w_up, w_down, layer, g, out_dtype, 0, tm, tf=MLP_TF_F32,
                              emit_w=True, out_rows=m_all)
    return mlp(xn, x, wu_b, wd_b, None, g, out_dtype, tm, m_all - tm, tf=MLP_TF_BF16,
               acc=(x0, xn0), out_rows=m_all, tm=tm)


def _rglru_gates(xc, wg_ref, ba, bx, lam):
    C = xc.shape[1]
    xcb = xc.astype(BF16)
    r_parts, i_parts = [], []
    for c in range(C // GATE_BLOCK):
        lo, hi = c * GATE_BLOCK, (c + 1) * GATE_BLOCK
        gates = _dot(xcb[:, lo:hi], wg_ref[c])
        r_parts.append(gates[:, :GATE_BLOCK])
        i_parts.append(gates[:, GATE_BLOCK:])
    r = _sigmoid(jnp.concatenate(r_parts, axis=1) + ba)
    ig = _sigmoid(jnp.concatenate(i_parts, axis=1) + bx)
    log_a = (-LRU_C * _softplus(-lam)) * r
    a = jnp.exp(log_a)
    u = jnp.sqrt(-jnp.tanh(log_a) * (a * a + 1.0)) * (ig * xc)
    return a, u


def _rglru_prompt_kernel(xr_ref, gr_ref, cw_ref, cb_ref, wg_ref, ba_ref, bx_ref, lam_ref,
                         y_ref, tail_ref, hlast_ref, xp_ref, a_ref, u_ref, h_ref):
    t = pl.program_id(1)
    tT, C = xr_ref.shape
    nk = cw_ref.shape[0]

    @pl.when(t == 0)
    def _():
        xp_ref[0:SUBLANES, :] = jnp.zeros((SUBLANES, C), F32)
        h_ref[...] = jnp.zeros_like(h_ref)

    x = xr_ref[...]
    xp_ref[SUBLANES:SUBLANES + tT, :] = x
    xc = cb_ref[...] + cw_ref[nk - 1:nk, :] * x
    for k in range(nk - 1):
        off = SUBLANES - (nk - 1) + k
        xc = xc + cw_ref[k:k + 1, :] * xp_ref[off:off + tT, :]
    xp_ref[0:SUBLANES, :] = x[tT - SUBLANES:tT, :]

    a, u = _rglru_gates(xc, wg_ref, ba_ref[...], bx_ref[...], lam_ref[...])
    a_ref[...] = a
    u_ref[...] = u

    rowi = lax.broadcasted_iota(jnp.int32, (SUBLANES, C), 0)

    def body(g, h):
        r0 = pl.multiple_of(g * SUBLANES, SUBLANES)
        a8 = a_ref[pl.ds(r0, SUBLANES), :]
        u8 = u_ref[pl.ds(r0, SUBLANES), :]
        for s in (1, 2, 4):
            a_sh = pltpu.roll(a8, s, 0)
            u_sh = pltpu.roll(u8, s, 0)
            m = rowi >= s
            u8 = jnp.where(m, a8 * u_sh + u8, u8)
            a8 = jnp.where(m, a8 * a_sh, a8)
        h8 = a8 * h + u8
        u_ref[pl.ds(r0, SUBLANES), :] = h8
        return h8[SUBLANES - 1:SUBLANES, :]

    h = lax.fori_loop(0, tT // SUBLANES, body, h_ref[...])
    h_ref[...] = h
    y_ref[...] = (u_ref[...] * _gelu_tanh(gr_ref[...])).astype(y_ref.dtype)

    @pl.when(t == pl.num_programs(1) - 1)
    def _():
        tail_ref[0] = x[tT - SUBLANES:tT, :]
        hlast_ref[0] = h


def rglru_prompt(proj, B, T, cw, cb, wg, ba, bx, lam):
    C = cw.shape[1]
    tT = _tile(T, 256)
    nT = T // tT
    fixed2 = lambda b, t: (0, 0)
    vec = pl.BlockSpec((1, C), fixed2)
    return pl.pallas_call(
        _rglru_prompt_kernel, grid=(B, nT),
        in_specs=[pl.BlockSpec((tT, C), lambda b, t: (b * nT + t, 0)),
                  pl.BlockSpec((tT, C), lambda b, t: (b * nT + t, 1)),
                  pl.BlockSpec(cw.shape, fixed2), vec,
                  pl.BlockSpec(wg.shape, lambda b, t: (0, 0, 0)), vec, vec, vec],
        out_specs=[pl.BlockSpec((tT, C), lambda b, t: (b * nT + t, 0)),
                   pl.BlockSpec((1, SUBLANES, C), lambda b, t: (b, 0, 0)),
                   pl.BlockSpec((1, 1, C), lambda b, t: (b, 0, 0))],
        out_shape=[jax.ShapeDtypeStruct((proj.shape[0], C), BF16),
                   jax.ShapeDtypeStruct((B, SUBLANES, C), F32),
                   jax.ShapeDtypeStruct((B, 1, C), F32)],
        scratch_shapes=[pltpu.VMEM((SUBLANES + tT, C), F32), pltpu.VMEM((tT, C), F32),
                        pltpu.VMEM((tT, C), F32), pltpu.VMEM((1, C), F32)],
        compiler_params=_cp(("parallel", "arbitrary"), 40), name="rglru_prompt",
    )(proj, proj, cw, cb.reshape(1, C), wg, ba.reshape(1, C), bx.reshape(1, C), lam.reshape(1, C))


CONF_STRIDE = 4
LANES = 128


def _conf_prompt_kernel(ac_ref, bc_ref, dw_ref, db_ref, lg_ref, lb_ref,
                        y_ref, tail_ref, gp_ref, yc_ref):
    t = pl.program_id(1)
    tT, C = ac_ref.shape
    nk = dw_ref.shape[0]
    pad = gp_ref.shape[1] - tT
    nslab = C // LANES
    span = CONF_STRIDE * SUBLANES

    @pl.when(t == 0)
    def _():
        gp_ref[:, 0:pad, :] = jnp.zeros((nslab, pad, LANES), F32)

    glu = ac_ref[...] * _sigmoid(bc_ref[...])
    for c in range(nslab):
        gp_ref[c, pad:pad + tT, :] = glu[:, c * LANES:(c + 1) * LANES]
    for c in range(nslab):
        lo = c * LANES
        for r0 in range(0, tT, span):
            accs = [jnp.broadcast_to(db_ref[:, lo:lo + LANES], (SUBLANES, LANES)) for _ in range(CONF_STRIDE)]
            for k in range(nk):
                w = dw_ref[k:k + 1, lo:lo + LANES]
                off = pad - (nk - 1) + k + r0
                for p in range(CONF_STRIDE):
                    accs[p] = accs[p] + gp_ref[c, pl.ds(off + p, SUBLANES, stride=CONF_STRIDE), :] * w
            for p in range(CONF_STRIDE):
                yc_ref[c, pl.ds(r0 + p, SUBLANES, stride=CONF_STRIDE), :] = accs[p]
    yc = jnp.concatenate([yc_ref[c] for c in range(nslab)], axis=1)
    yn = _layernorm(yc, lg_ref[...], lb_ref[...])
    y_ref[...] = (yn * _sigmoid(yn)).astype(y_ref.dtype)

    @pl.when(t == pl.num_programs(1) - 1)
    def _():
        tail_ref[0] = jnp.concatenate([gp_ref[c, tT:tT + pad, :] for c in range(nslab)], axis=1)

    gp_ref[:, 0:pad, :] = gp_ref[:, tT:tT + pad, :]


def conf_prompt(proj, B, T, dw, db, lg, lb):
    nk, C = dw.shape
    tT = _tile(T, 256)
    nT = T // tT
    pad = -(-(nk - 1) // SUBLANES) * SUBLANES
    assert tT % (CONF_STRIDE * SUBLANES) == 0 and C % LANES == 0
    fixed2 = lambda b, t: (0, 0)
    vec = pl.BlockSpec((1, C), fixed2)
    return pl.pallas_call(
        _conf_prompt_kernel, grid=(B, nT),
        in_specs=[pl.BlockSpec((tT, C), lambda b, t: (b * nT + t, 2)),
                  pl.BlockSpec((tT, C), lambda b, t: (b * nT + t, 3)),
                  pl.BlockSpec((nk, C), fixed2), vec, vec, vec],
        out_specs=[pl.BlockSpec((tT, C), lambda b, t: (b * nT + t, 0)),
                   pl.BlockSpec((1, pad, C), lambda b, t: (b, 0, 0))],
        out_shape=[jax.ShapeDtypeStruct((proj.shape[0], C), BF16), jax.ShapeDtypeStruct((B, pad, C), F32)],
        scratch_shapes=[pltpu.VMEM((C // LANES, pad + tT, LANES), F32),
                        pltpu.VMEM((C // LANES, tT, LANES), F32)],
        compiler_params=_cp(("parallel", "arbitrary"), 32), name="conf_prompt",
    )(proj, proj, dw, db.reshape(1, C), lg.reshape(1, C), lb.reshape(1, C))


def _ab_sample_kernel(*refs, has_acc):
    (proj_ref, rconv_ref, h0_ref, cconv_ref,
     cw_ref, cb_ref, wg_ref, ba_ref, bx_ref, lam_ref, dw_ref, db_ref, lg_ref, lb_ref) = refs[:14]
    yr_ref, yc_ref, rconv_o, h_o, cconv_o = refs[14 + int(has_acc):]
    C = h0_ref.shape[1]
    nk = cw_ref.shape[0]
    nd = dw_ref.shape[0]
    x_r = proj_ref[:, 0:C]
    g_r = proj_ref[:, C:2 * C]
    glu = proj_ref[:, 2 * C:3 * C] * _sigmoid(proj_ref[:, 3 * C:4 * C])

    xc = cb_ref[...] + cw_ref[nk - 1:nk, :] * x_r
    for k in range(nk - 1):
        xc = xc + cw_ref[k:k + 1, :] * rconv_ref[k]
    for k in range(nk - 2):
        rconv_o[k] = rconv_ref[k + 1]
    rconv_o[nk - 2] = x_r

    yc = db_ref[...] + dw_ref[nd - 1:nd, :] * glu
    for k in range(nd - 1):
        yc = yc + dw_ref[k:k + 1, :] * cconv_ref[k]
    for k in range(nd - 2):
        cconv_o[k] = cconv_ref[k + 1]
    cconv_o[nd - 2] = glu

    a, u = _rglru_gates(xc, wg_ref, ba_ref[...], bx_ref[...], lam_ref[...])
    h = a * h0_ref[...] + u
    h_o[...] = h
    yr_ref[...] = h * _gelu_tanh(g_r)
    yn = _layernorm(yc, lg_ref[...], lb_ref[...])
    yc_ref[...] = yn * _sigmoid(yn)


def ab_sample(proj, row0, rconv_t, h0, cconv_t, layer, cconv_acc, cw, cb, wg, ba, bx, lam, dw, db, lg, lb):
    _, B, C = h0.shape
    bt = _tile(B, 32)
    row = lambda i: (i, 0)
    fixed = lambda i: (0, 0)
    vec = pl.BlockSpec((1, C), fixed)
    nr, nc = rconv_t.shape[1], cconv_t.shape[1]
    st_c = pl.BlockSpec((None, nc, bt, C), lambda i: (layer, 0, i, 0))
    act = pl.BlockSpec((bt, C), row)
    has_acc = cconv_acc is not None
    assert row0 % bt == 0
    off = row0 // bt
    in_specs = [pl.BlockSpec((bt, proj.shape[1]), lambda i: (i + off, 0)),
                pl.BlockSpec((None, nr, bt, C), lambda i: (layer, 0, i, 0)),
                pl.BlockSpec((None, bt, C), lambda i: (layer, i, 0)), st_c,
                pl.BlockSpec(cw.shape, fixed), vec, pl.BlockSpec(wg.shape, lambda i: (0, 0, 0)),
                vec, vec, vec, pl.BlockSpec(dw.shape, fixed), vec, vec, vec]
    args = [proj, rconv_t, h0, cconv_t, cw, cb.reshape(1, C), wg, ba.reshape(1, C), bx.reshape(1, C),
            lam.reshape(1, C), dw, db.reshape(1, C), lg.reshape(1, C), lb.reshape(1, C)]
    if has_acc:
        in_specs.append(pl.BlockSpec(memory_space=pl.ANY))
        args.append(cconv_acc)
    return pl.pallas_call(
        functools.partial(_ab_sample_kernel, has_acc=has_acc), grid=(B // bt,),
        in_specs=in_specs,
        out_specs=[act, act, pl.BlockSpec((nr, bt, C), lambda i: (0, i, 0)), act, st_c],
        out_shape=[jax.ShapeDtypeStruct((B, C), F32), jax.ShapeDtypeStruct((B, C), F32),
                   jax.ShapeDtypeStruct((nr, B, C), F32), jax.ShapeDtypeStruct((B, C), F32),
                   jax.ShapeDtypeStruct(cconv_t.shape, F32)],
        input_output_aliases={len(args) - 1: 4} if has_acc else {},
        compiler_params=_cp(("parallel",), 40), name="ab_sample",
    )(*args)


def _mlstm_prompt_kernel(q_ref, k_ref, v_ref, o_ref, g_ref, bif_ref, hng_ref,
                         y_ref, C_ref, n_ref, m_ref, *, heads):
    c = pl.program_id(1)
    L = q_ref.shape[0]
    H = heads
    DK = q_ref.shape[1] // H
    DV = v_ref.shape[1] // H
    scale = DK ** -0.5

    @pl.when(c == 0)
    def _():
        C_ref[...] = jnp.zeros_like(C_ref)
        n_ref[...] = jnp.zeros_like(n_ref)
        m_ref[...] = jnp.zeros_like(m_ref)

    ti = lax.broadcasted_iota(jnp.int32, (L, L), 0)
    si = lax.broadcasted_iota(jnp.int32, (L, L), 1)
    causal = si <= ti
    tri_lo = causal.astype(BF16)
    tri_up = (ti <= si).astype(BF16)

    gates = g_ref[...] + bif_ref[...]
    lf = _log_sigmoid(gates)
    gates_t = gates.T
    lf_t = lf.T
    hi, mid, lo = _split3(lf)
    b_cols = (_dot(tri_lo, lo) + _dot(tri_lo, mid)) + _dot(tri_lo, hi)
    hi, mid, lo = _split3(lf_t)
    b_rows = (_dot(lo, tri_up) + _dot(mid, tri_up)) + _dot(hi, tri_up)

    m_all = m_ref[0]
    n_all = n_ref[0]
    hs = range(H)
    dk = lambda h: slice(h * DK, (h + 1) * DK)
    dv = lambda h: slice(h * DV, (h + 1) * DV)
    ig_col = [gates[:, h:h + 1] for h in hs]
    ig_row = [gates_t[h:h + 1, :] for h in hs]
    b_col = [b_cols[:, H + h:H + h + 1] for h in hs]
    b_row = [b_rows[H + h:H + h + 1, :] for h in hs]
    m_prev = [m_all[h:h + 1, 0:1] for h in hs]
    n_row = [n_all[h:h + 1, :] for h in hs]

    qs = [q_ref[:, dk(h)] * scale for h in hs]
    qb = [qs[h].astype(BF16) for h in hs]
    kb = [k_ref[:, dk(h)].astype(BF16) for h in hs]
    vb = [v_ref[:, dv(h)].astype(BF16) for h in hs]
    dmat = [jnp.where(causal, b_col[h] - b_row[h] + ig_row[h], -jnp.inf) for h in hs]
    inter = [b_col[h] + m_prev[h] for h in hs]
    m_t = [jnp.maximum(inter[h], jnp.max(dmat[h], axis=-1, keepdims=True)) for h in hs]
    qk = [lax.dot_general(qb[h], kb[h], (((1,), (1,)), ((), ())), preferred_element_type=F32) for h in hs]
    s = [qk[h] * jnp.exp(dmat[h] - m_t[h]) for h in hs]
    scale_in = [jnp.exp(inter[h] - m_t[h]) for h in hs]
    qC = [_dot(qb[h], C_ref[0, h].astype(BF16)) for h in hs]
    num = [_dot(s[h].astype(BF16), vb[h]) + scale_in[h] * qC[h] for h in hs]
    den = [jnp.sum(s[h], axis=-1, keepdims=True)
           + scale_in[h] * jnp.sum(qs[h] * n_row[h], axis=-1, keepdims=True) for h in hs]
    hh = [num[h] / jnp.maximum(jnp.abs(den[h]), jnp.exp(-m_t[h])) for h in hs]
    for h in hs:
        hn = hh[h] * lax.rsqrt(jnp.mean(hh[h] * hh[h], axis=-1, keepdims=True) + EPS)
        y = (hn * hng_ref[:, dv(h)]) * _sigmoid(o_ref[:, dv(h)])
        y_ref[:, dv(h)] = y.astype(y_ref.dtype)
    b_last = [b_col[h][L - 1:L, :] for h in hs]
    g_row = [b_last[h] - b_row[h] + ig_row[h] for h in hs]
    g_col = [b_last[h] - b_col[h] + ig_col[h] for h in hs]
    m_new = [jnp.maximum(b_last[h] + m_prev[h], jnp.max(g_row[h], axis=-1, keepdims=True)) for h in hs]
    decay = [jnp.exp(b_last[h] + m_prev[h] - m_new[h]) for h in hs]
    kw = [k_ref[:, dk(h)] * jnp.exp(g_col[h] - m_new[h]) for h in hs]
    for h in hs:
        C_ref[0, h] = decay[h] * C_ref[0, h] + lax.dot_general(
            kw[h].astype(BF16), vb[h], (((0,), (0,)), ((), ())), preferred_element_type=F32)
    n_ref[0] = jnp.concatenate([decay[h] * n_row[h] + jnp.sum(kw[h], axis=0, keepdims=True) for h in hs], axis=0)
    m_ref[0] = jnp.concatenate([jnp.broadcast_to(m_new[h], (1, m_ref.shape[2])) for h in hs], axis=0)


def mlstm_prompt(proj, gates, bif_row, hng, B, T, heads):
    QK = proj.shape[1] // 6
    V = 2 * QK
    H = heads
    DK, DV = QK // H, V // H
    L = ML_CHUNK if T % ML_CHUNK == 0 else T
    nc = T // L
    rowmap = lambda j: (lambda b, c: (b * nc + c, j))
    fixed2 = lambda b, c: (0, 0)
    return pl.pallas_call(
        functools.partial(_mlstm_prompt_kernel, heads=H), grid=(B, nc),
        in_specs=[pl.BlockSpec((L, QK), rowmap(0)), pl.BlockSpec((L, QK), rowmap(1)),
                  pl.BlockSpec((L, V), rowmap(1)), pl.BlockSpec((L, V), rowmap(2)),
                  pl.BlockSpec((L, LANES), rowmap(0)), pl.BlockSpec((1, LANES), fixed2),
                  pl.BlockSpec((1, V), fixed2)],
        out_specs=[pl.BlockSpec((L, V), rowmap(0)),
                   pl.BlockSpec((1, H, DK, DV), lambda b, c: (b, 0, 0, 0)),
                   pl.BlockSpec((1, H, DK), lambda b, c: (b, 0, 0)),
                   pl.BlockSpec((1, H, LANES), lambda b, c: (b, 0, 0))],
        out_shape=[jax.ShapeDtypeStruct((proj.shape[0], V), BF16),
                   jax.ShapeDtypeStruct((B, H, DK, DV), F32),
                   jax.ShapeDtypeStruct((B, H, DK), F32),
                   jax.ShapeDtypeStruct((B, H, LANES), F32)],
        compiler_params=_cp(("parallel", "arbitrary"), 40), name="mlstm_prompt",
    )(proj, proj, proj, proj, gates, bif_row, hng.reshape(1, V))


def _mlstm_gate_terms(ig, fg, m):
    lf = _log_sigmoid(fg)
    inter = lf + m
    m_t = jnp.maximum(inter, ig)
    return m_t, jnp.exp(ig - m_t), jnp.exp(inter - m_t)


def _mlstm_sample_kernel(*refs, has_acc):
    (qr_ref, kr_ref, v_ref, o_ref, gr_ref, gc_ref, bifr_ref, bifc_ref,
     hng_ref, C_ref, n_ref) = refs[:11]
    y_ref, Co_ref, no_ref, mo_ref = refs[11 + int(has_acc):]
    bb, H, DK = qr_ref.shape
    scale = DK ** -0.5
    bifr = bifr_ref[...]
    bifc = bifc_ref[...]
    for j in range(bb):
        gr = gr_ref[j]
        gc = gc_ref[j]
        mt_r, es_r, sc_r = _mlstm_gate_terms(gr[:, 0:H] + bifr[:, 0:H], gr[:, H:2 * H] + bifr[:, H:2 * H],
                                             gr[:, 2 * H:3 * H])
        mt_c, es_c, sc_c = _mlstm_gate_terms(gc[0:H, :] + bifc[0:H, :], gc[H:2 * H, :] + bifc[H:2 * H, :],
                                             gc[2 * H:3 * H, :])
        qr = qr_ref[j] * scale
        kr = kr_ref[j]
        n = n_ref[j]
        v = v_ref[j]
        qk = jnp.sum(qr * kr, axis=1, keepdims=True)
        qn = jnp.sum(qr * n, axis=1, keepdims=True)
        qt = qr.T
        kts = kr.T * es_r
        rows = []
        for h in range(H):
            Ch = C_ref[j, h]
            rows.append(jnp.sum(Ch * qt[:, h:h + 1], axis=0, keepdims=True))
            Co_ref[j, h] = sc_r[:, h:h + 1] * Ch + kts[:, h:h + 1] * v[h:h + 1, :]
        qC = jnp.concatenate(rows, axis=0)
        s = qk * es_c
        num = s * v + sc_c * qC
        den = s + sc_c * qn
        hh = num / jnp.maximum(jnp.abs(den), jnp.exp(-mt_c))
        hn = hh * lax.rsqrt(jnp.mean(hh * hh, axis=-1, keepdims=True) + EPS)
        y_ref[j] = (hn * hng_ref[...]) * _sigmoid(o_ref[j])
        no_ref[j] = sc_c * n + es_c * kr
        mo_ref[j] = mt_r


def mlstm_sample(q, k, v, o, gates, m, bif, hng, C, n, layer, C_acc):
    B, H, DK = q.shape
    DV = v.shape[2]
    bb = _tile(B, 8)
    gm = jnp.concatenate([gates, m], axis=1)
    b3 = lambda i: (i, 0, 0)
    fixed = lambda i: (0, 0)
    blk = lambda a: pl.BlockSpec((bb,) + a.shape[1:], b3)
    C_spec = pl.BlockSpec((None, bb, H, DK, DV), lambda i: (layer, i, 0, 0, 0))
    has_acc = C_acc is not None
    in_specs = [blk(q), blk(k), blk(v), blk(o),
                pl.BlockSpec((bb, 1, 3 * H), b3), pl.BlockSpec((bb, 3 * H, 1), b3),
                pl.BlockSpec((1, 2 * H), fixed), pl.BlockSpec((2 * H, 1), fixed),
                pl.BlockSpec((H, DV), fixed), C_spec,
                pl.BlockSpec((None, bb, H, DK), lambda i: (layer, i, 0, 0))]
    args = [q, k, v, o, gm.reshape(B, 1, 3 * H), gm.reshape(B, 3 * H, 1),
            bif.reshape(1, 2 * H), bif.reshape(2 * H, 1), hng.reshape(H, DV), C, n]
    if has_acc:
        in_specs.append(pl.BlockSpec(memory_space=pl.ANY))
        args.append(C_acc)
    return pl.pallas_call(
        functools.partial(_mlstm_sample_kernel, has_acc=has_acc), grid=(B // bb,),
        in_specs=in_specs,
        out_specs=[blk(v), C_spec, pl.BlockSpec((bb, H, DK), b3), pl.BlockSpec((bb, 1, H), b3)],
        out_shape=[jax.ShapeDtypeStruct((B, H, DV), F32), jax.ShapeDtypeStruct(C.shape, F32),
                   jax.ShapeDtypeStruct((B, H, DK), F32), jax.ShapeDtypeStruct((B, 1, H), F32)],
        input_output_aliases={len(args) - 1: 1} if has_acc else {},
        compiler_params=_cp(("parallel",), 48), name="mlstm_sample",
    )(*args)


def _blockdiag(w, per):
    H, d, _ = w.shape
    w4 = w.reshape(H // per, per, d, d)
    eye = jnp.eye(per, dtype=w.dtype)
    return (w4[:, :, :, None, :] * eye[None, :, None, :, None]).reshape(H // per, per * d, per * d)


def _prep_weights(P):
    W = {}
    W['ab_w_out'] = P['ab_w_out'].astype(BF16)
    W['ml_w_out'] = P['ml_w_out'].astype(BF16)
    heads = P['ml_b_if'].shape[1] // 2
    n_main = P['ml_w_in'].shape[2] - 2 * heads
    W['ml_w_in_t'] = jnp.swapaxes(P['ml_w_in'], 1, 2)
    W['ml_w_gate'] = jnp.pad(P['ml_w_in'][:, :, n_main:], ((0, 0), (0, 0), (0, LANES - 2 * heads)))
    W['ml_bif_row'] = jnp.pad(P['ml_b_if'], ((0, 0), (0, LANES - 2 * heads)))[:, None, :]
    d = P['rglru_wa'].shape[2]
    per = GATE_BLOCK // d
    wa = jax.vmap(lambda w: _blockdiag(w, per))(P['rglru_wa'])
    wx = jax.vmap(lambda w: _blockdiag(w, per))(P['rglru_wx'])
    W['rglru_wg'] = jnp.concatenate([wa, wx], axis=-1).astype(BF16)
    return W


def _set_rows(y_all, y_rows, row0):
    return lax.dynamic_update_slice(y_all, y_rows.astype(y_all.dtype), (row0, 0))


def _run_trunk(xp, xs, P, W, Bp, Tp, states):
    depth = P['mlp_norm'].shape[0]
    heads = P['ml_b_if'].shape[1] // 2
    n_main = P['ml_w_in'].shape[2] - 2 * heads
    Mp, Bs = xp.shape[0], xs.shape[0]
    m_all = Mp + Bs
    p_rconv, p_rh, p_cconv, p_C, p_n, p_m = [], [], [], [], [], []
    s_rconv, s_rh, s_n, s_m = [], [], [], []
    cconv_acc = None
    C_acc = None
    x, xn = entry_norm(xs, P['ab_norm'][0], m_all, Mp, entry_norm(xp, P['ab_norm'][0], m_all, 0, None))
    for l in range(depth):
        i = l // 2
        if l % 2 == 0:
            proj = matmul(xn, P['ab_w_in'], i, P['ab_w_in'].shape[2], 1024)
            rg = (P['rglru_conv_w'][i], P['rglru_conv_b'][i], W['rglru_wg'][i], P['rglru_ba'][i],
                  P['rglru_bx'][i], P['rglru_lambda'][i])
            cf = (P['conf_dw_w'][i], P['conf_dw_b'][i], P['conf_ln_g'][i], P['conf_ln_b'][i])
            y_r, tail_r, h_last = rglru_prompt(proj, Bp, Tp, *rg)
            y_c, tail_c = conf_prompt(proj, Bp, Tp, *cf)
            nr = P['rglru_conv_w'].shape[1] - 1
            ncf = P['conf_dw_w'].shape[1] - 1
            p_rconv.append(tail_r[:, tail_r.shape[1] - nr:])
            p_rh.append(h_last[:, 0])
            p_cconv.append(tail_c[:, tail_c.shape[1] - ncf:])
            ys_r, ys_c, rconv_new, h_new, cconv_acc = ab_sample(proj, Mp, states[0], states[1], states[2], i,
                                                                cconv_acc, *rg, *cf)
            s_rconv.append(rconv_new); s_rh.append(h_new)
            acts = [_set_rows(y_r, ys_r, Mp), _set_rows(y_c, ys_c, Mp)]
            x, xn = out_proj(acts, W['ab_w_out'], i, x, P['mlp_norm'][l])
        else:
            proj = matmul(xn, W['ml_w_in_t'], i, n_main, 1024, transposed=True)
            gates = matmul(xn, W['ml_w_gate'], i, LANES, LANES)
            hng = P['ml_hnorm_g'][i]
            y, C, n, m = mlstm_prompt(proj, gates, W['ml_bif_row'][i], hng, Bp, Tp, heads)
            p_C.append(C); p_n.append(n); p_m.append(m[:, :, 0])
            QK = n_main // 6
            V = 2 * QK
            ps = proj[Mp:]
            q = ps[:, :QK].reshape(Bs, heads, QK // heads)
            k = ps[:, QK:2 * QK].reshape(Bs, heads, QK // heads)
            v = ps[:, 2 * QK:2 * QK + V].reshape(Bs, heads, V // heads)
            o = ps[:, 2 * QK + V:].reshape(Bs, heads, V // heads)
            ys, C_acc, n, m = mlstm_sample(q, k, v, o, gates[Mp:, :2 * heads], states[5][i], P['ml_b_if'][i],
                                           hng, states[3], states[4], i, C_acc)
            s_n.append(n); s_m.append(m[:, 0])
            x, xn = out_proj([_set_rows(y, ys.reshape(Bs, V), Mp)], W['ml_w_out'], i, x, P['mlp_norm'][l])
        if l < depth - 1:
            g_next = P['ab_norm'][(l + 1) // 2] if (l + 1) % 2 == 0 else P['ml_norm'][(l + 1) // 2]
            x, xn = mlp_all_rows(xn, x, P['mlp_w_up'], P['mlp_w_down'], l, g_next, BF16)
    last = depth - 1
    fin = (P['final_norm'], F32)
    tm0 = _row_tile(Mp, MM_ROWS_CAP)
    yp, wu_b, wd_b = mlp(xn, x, P['mlp_w_up'], P['mlp_w_down'], last, *fin, 0, tm0,
                         tf=MLP_TF_F32, emit_x=False, emit_w=True, out_rows=Mp)
    if tm0 < Mp:
        (yp,) = mlp(xn, x, wu_b, wd_b, None, *fin, tm0, Mp - tm0, tf=MLP_TF_BF16, emit_x=False,
                    acc=(yp,), out_rows=Mp, tm=tm0)
    (ys,) = mlp(xn, x, wu_b, wd_b, None, *fin, Mp, Bs, tf=MLP_TF_BF16, emit_x=False)
    prompt_out = (jnp.stack(p_rconv), jnp.stack(p_rh), jnp.stack(p_cconv),
                  jnp.stack(p_C), jnp.stack(p_n), jnp.stack(p_m))
    sample_out = (jnp.swapaxes(jnp.stack(s_rconv), 1, 2), jnp.stack(s_rh), jnp.swapaxes(cconv_acc, 1, 2),
                  C_acc, jnp.stack(s_n), jnp.stack(s_m))
    return yp, ys, prompt_out, sample_out


def kernel(x_prompt, x_sample, state_rglru_conv, state_rglru_h, state_conf_conv, state_mlstm_C, state_mlstm_n, state_mlstm_m, ab_norm, ab_w_in, rglru_conv_w, rglru_conv_b, rglru_wa, rglru_ba, rglru_wx, rglru_bx, rglru_lambda, conf_dw_w, conf_dw_b, conf_ln_g, conf_ln_b, ab_w_out, ml_norm, ml_w_in, ml_b_if, ml_hnorm_g, ml_w_out, mlp_norm, mlp_w_up, mlp_w_down, final_norm):
    P = {'ab_norm': ab_norm, 'ab_w_in': ab_w_in, 'rglru_conv_w': rglru_conv_w, 'rglru_conv_b': rglru_conv_b,
         'rglru_wa': rglru_wa, 'rglru_ba': rglru_ba, 'rglru_wx': rglru_wx, 'rglru_bx': rglru_bx,
         'rglru_lambda': rglru_lambda, 'conf_dw_w': conf_dw_w, 'conf_dw_b': conf_dw_b,
         'conf_ln_g': conf_ln_g, 'conf_ln_b': conf_ln_b, 'ab_w_out': ab_w_out, 'ml_norm': ml_norm,
         'ml_w_in': ml_w_in, 'ml_b_if': ml_b_if, 'ml_hnorm_g': ml_hnorm_g, 'ml_w_out': ml_w_out,
         'mlp_norm': mlp_norm, 'mlp_w_up': mlp_w_up, 'mlp_w_down': mlp_w_down, 'final_norm': final_norm}
    W = _prep_weights(P)
    Bp, Tp, D = x_prompt.shape
    Bs, Ts, _ = x_sample.shape
    assert Ts == 1, "the sample group advances one token per sequence"
    states = (jnp.swapaxes(state_rglru_conv, 1, 2), state_rglru_h, jnp.swapaxes(state_conf_conv, 1, 2),
              state_mlstm_C, state_mlstm_n, state_mlstm_m)
    yp, ys, p_states, s_states = _run_trunk(x_prompt.reshape(Bp * Tp, D), x_sample.reshape(Bs, D),
                                            P, W, Bp, Tp, states)
    return (yp.reshape(Bp, Tp, D), ys.reshape(Bs, Ts, D), *p_states, *s_states)
```
